```python
import math
import jax
import jax.numpy as jnp
from jax import lax
import numpy as np

D_MODEL = 1024
BATCH = 8
SEQ = 8192
DEPTH = 1
DEC_BATCH = 2
DEC_SEQ = 8192
PAST_LEN = 128

A_HEADS = 8
A_HEAD_DIM = 64
A_WIDTH = A_HEADS * A_HEAD_DIM
DECAY_RANK = 64
ICLR_RANK = 64
A_COLS = 4 * A_WIDTH + 2 * DECAY_RANK + 2 * ICLR_RANK
GN_EPS = 64e-5

B_GROUPS = ((128, 1), (512, 4), (2048, 16))
N_GROUPS = len(B_GROUPS)
B_HEADS_PER_GROUP = 4
B_HEAD_DIM = 128
B_HEADS = N_GROUPS * B_HEADS_PER_GROUP
B_QKV = B_HEADS * B_HEAD_DIM
B_WIDTH = B_HEADS_PER_GROUP * B_HEAD_DIM
B_COLS = 3 * B_QKV + B_WIDTH
Q_BLOCK = 128

N_BUCKETS = 32
MAX_EXACT = 8
MAX_DISTANCE = 1024

IN_COLS = A_COLS + B_COLS + 2 * D_MODEL
RMS_EPS = 1e-6
NEG = -1e30

kernel_name = 'hybrid_rwkv7_dilated_encoder'


def _rmsnorm(z):
    z = z.astype(jnp.float32)
    return z * lax.rsqrt(jnp.mean(z * z, axis=-1, keepdims=True) + RMS_EPS)


def _rwkv7_step(S, inp):
    r, w, k, v, a, b = inp
    Sa = jnp.einsum('dbhvk,dbhk->dbhv', S, a)
    S = S * w[..., None, :] + Sa[..., :, None] * b[..., None, :] + v[..., :, None] * k[..., None, :]
    return S, jnp.einsum('dbhvk,dbhk->dbhv', S, r)


def rwkv7_bidirectional(pa, mu, w0, w2, a0, a2, k_k, k_a, r_k, lnx_w, lnx_b):
    Bn, T, _ = pa.shape
    pa = pa.astype(jnp.float32)
    prev = jnp.pad(pa, ((0, 0), (1, 0), (0, 0)))[:, :-1]
    nxt = jnp.pad(pa, ((0, 0), (0, 1), (0, 0)))[:, 1:]
    pa = pa + mu[0] * (prev - pa) + mu[1] * (nxt - pa)
    r, k, v, g, wlo, alo = jnp.split(
        pa, [A_WIDTH, 2 * A_WIDTH, 3 * A_WIDTH, 4 * A_WIDTH, 4 * A_WIDTH + 2 * DECAY_RANK], axis=-1)
    wlo = wlo.reshape(Bn, T, 2, DECAY_RANK)
    alo = alo.reshape(Bn, T, 2, ICLR_RANK)
    wlog = -jax.nn.softplus(-(w0 + jnp.einsum('btdr,drc->btdc', jnp.tanh(wlo), w2))) - 0.5
    decay = jnp.exp(-jnp.exp(wlog))
    a = jax.nn.sigmoid(a0 + jnp.einsum('btdr,drc->btdc', alo, a2))
    kk = (k * k_k).reshape(Bn, T, A_HEADS, A_HEAD_DIM)
    kk = kk / jnp.maximum(jnp.sqrt(jnp.sum(kk * kk, axis=-1, keepdims=True)), 1e-12)
    kk = kk.reshape(Bn, T, A_WIDTH)
    k_dir = k[:, :, None] * (1.0 + (a - 1.0) * k_a)
    b_dir = kk[:, :, None] * a

    def both(z):
        return jnp.broadcast_to(z[:, :, None], (Bn, T, 2, A_WIDTH))

    def time_major(z):
        z = jnp.stack([z[:, :, 0], z[:, ::-1, 1]], axis=0)
        return z.reshape(2, Bn, T, A_HEADS, A_HEAD_DIM).transpose(2, 0, 1, 3, 4)

    xs = (time_major(both(r)), time_major(decay), time_major(k_dir),
          time_major(both(v)), time_major(both(-kk)), time_major(b_dir))
    S0 = jnp.zeros((2, Bn, A_HEADS, A_HEAD_DIM, A_HEAD_DIM), jnp.float32)
    _, y = lax.scan(_rwkv7_step, S0, xs)
    y = (y[:, 0] + y[::-1, 1]).transpose(1, 0, 2, 3)
    mean = jnp.mean(y, axis=-1, keepdims=True)
    var = jnp.mean(jnp.square(y - mean), axis=-1, keepdims=True)
    yn = ((y - mean) * lax.rsqrt(var + GN_EPS)).reshape(Bn, T, A_WIDTH) * lnx_w + lnx_b
    bonus = jnp.sum((r[:, :, None] * k_dir * r_k.reshape(A_WIDTH)).reshape(Bn, T, 2, A_HEADS, A_HEAD_DIM),
                    axis=(2, 4))[..., None] * v.reshape(Bn, T, A_HEADS, A_HEAD_DIM)
    return (yn + bonus.reshape(Bn, T, A_WIDTH)) * jax.nn.silu(g)


def t5_bucket(rel):
    half_b = N_BUCKETS // 2
    ret = jnp.where(rel > 0, half_b, 0)
    n = jnp.abs(rel)
    large = MAX_EXACT + (jnp.log(jnp.maximum(n, 1).astype(jnp.float32) / MAX_EXACT)
                         / math.log(MAX_DISTANCE / MAX_EXACT) * (half_b - MAX_EXACT)).astype(jnp.int32)
    large = jnp.minimum(large, half_b - 1)
    return ret + jnp.where(n < MAX_EXACT, n, large)


def band_bias(table, dil, half):
    i = jnp.arange(Q_BLOCK)[:, None]
    j = jnp.arange(2 * Q_BLOCK)[None, :]
    rel = j - half - i
    b = table[t5_bucket(rel * dil)].astype(jnp.float32)
    b = jnp.where((jnp.abs(rel) <= half)[..., None], b, NEG)
    return b.transpose(2, 0, 1)


def by_residue(z, dil):
    Bn, T, H, E = z.shape
    return z.reshape(Bn, T // dil, dil, H, E).transpose(0, 2, 3, 1, 4).reshape(Bn * dil, H, T // dil, E)


def from_residue(z, Bn, dil):
    N, H, L, E = z.shape
    return z.reshape(Bn, dil, H, L, E).transpose(0, 3, 1, 2, 4).reshape(Bn, L * dil, H, E)


def banded_attention(q, k, v, bias, half):
    N, H, L, E = q.shape
    nb = -(-L // Q_BLOCK)
    Lp = nb * Q_BLOCK
    qb = jnp.pad(q, ((0, 0), (0, 0), (0, Lp - L), (0, 0))).reshape(N, H, nb, Q_BLOCK, E)

    def windows(z):
        zp = jnp.pad(z, ((0, 0), (0, 0), (half, Lp - L + Q_BLOCK - half), (0, 0)))
        zp = zp.reshape(N, H, nb + 1, Q_BLOCK, E)
        return jnp.concatenate([zp[:, :, :-1], zp[:, :, 1:]], axis=3)

    kw, vw = windows(k), windows(v)
    pos = jnp.arange(nb)[:, None] * Q_BLOCK + jnp.arange(2 * Q_BLOCK)[None, :] - half
    valid = (pos >= 0) & (pos < L)
    s = jnp.einsum('nhbqe,nhbke->nhbqk', qb, kw) * (B_HEAD_DIM ** -0.5) + bias[:, None]
    s = jnp.where(valid[:, None, :], s, NEG)
    m = jnp.max(s, axis=-1, keepdims=True)
    p = jnp.exp(s - m)
    den = jnp.sum(p, axis=-1, keepdims=True)
    o = jnp.einsum('nhbqk,nhbke->nhbqe', p, vw) / den
    lse = (m + jnp.log(den))[..., 0]
    return o.reshape(N, H, Lp, E)[:, :, :L], lse.reshape(N, H, Lp)[:, :, :L]


def dilated_attention(pb, q_gain, k_gain, rel_bias):
    Bn, T, _ = pb.shape
    q, k, v, g = jnp.split(pb, [B_QKV, 2 * B_QKV, 3 * B_QKV], axis=-1)
    shp = (Bn, T, N_GROUPS, B_HEADS_PER_GROUP, B_HEAD_DIM)
    q = _rmsnorm(q.reshape(shp)) * q_gain
    k = _rmsnorm(k.reshape(shp)) * k_gain
    v = v.reshape(shp).astype(jnp.float32)
    outs, lses = [], []
    for gi, (window, dil) in enumerate(B_GROUPS):
        half = window // (2 * dil)
        bias = band_bias(rel_bias[:, gi * B_HEADS_PER_GROUP:(gi + 1) * B_HEADS_PER_GROUP], dil, half)
        o, lse = banded_attention(by_residue(q[:, :, gi], dil), by_residue(k[:, :, gi], dil),
                                  by_residue(v[:, :, gi], dil), bias, half)
        outs.append(from_residue(o, Bn, dil))
        lses.append(from_residue(lse[..., None], Bn, dil)[..., 0])
    alpha = jax.nn.softmax(jnp.stack(lses, axis=2), axis=2)
    o = jnp.einsum('btgh,btghe->bthe', alpha, jnp.stack(outs, axis=2))
    return o.reshape(Bn, T, B_WIDTH) * jax.nn.silu(g.astype(jnp.float32))


def encoder_layer(x, c, rel_bias, norm_w, w_ada, b_ada, w_in, shift_mu, w0, w2, a0, a2,
                  k_k, k_a, r_k, lnx_w, lnx_b, q_gain, k_gain, w_out_a, w_out_b, w_out):
    mod = jax.nn.silu(c.astype(jnp.float32)) @ w_ada + b_ada
    shift, scale, gate = jnp.split(mod, 3, axis=-1)
    h = _rmsnorm(x) * norm_w * (1.0 + scale[:, None]) + shift[:, None]
    p = h @ w_in
    pa, pb, pg = jnp.split(p, [A_COLS, A_COLS + B_COLS], axis=-1)
    ya = rwkv7_bidirectional(pa, shift_mu, w0, w2, a0, a2, k_k, k_a, r_k, lnx_w, lnx_b)
    yb = dilated_attention(pb, q_gain, k_gain, rel_bias)
    gates = jax.nn.sigmoid(pg.astype(jnp.float32))
    merged = gates[..., :D_MODEL] * (ya @ w_out_a) + gates[..., D_MODEL:] * (yb @ w_out_b)
    return (x + gate[:, None] * (merged @ w_out)).astype(x.dtype)


def setup_inputs(seed: int = 0) -> dict:
    key = jax.random.key(seed)
    ks = jax.random.split(key, 24)
    f32 = jnp.float32

    def nrm(k, shape, s):
        return s * jax.random.normal(k, shape, f32)

    L = DEPTH
    return {
        'x_prompt': nrm(ks[0], (BATCH, SEQ, D_MODEL), 1.0),
        'x_sample': nrm(ks[1], (DEC_BATCH, DEC_SEQ, D_MODEL), 1.0),
        'c_prompt': nrm(ks[2], (BATCH, D_MODEL), 1.0),
        'c_sample': nrm(ks[3], (DEC_BATCH, D_MODEL), 1.0),
        'rel_bias': nrm(ks[4], (N_BUCKETS, B_HEADS), 0.5),
        'norm_w': 1.0 + nrm(ks[5], (L, D_MODEL), 0.02),
        'w_ada': nrm(ks[6], (L, D_MODEL, 3 * D_MODEL), D_MODEL ** -0.5),
        'b_ada': nrm(ks[7], (L, 3 * D_MODEL), 0.02),
        'w_in': nrm(ks[8], (L, D_MODEL, IN_COLS), D_MODEL ** -0.5),
        'shift_mu': jax.random.uniform(ks[9], (L, 2, A_COLS), f32, 0.0, 0.5),
        'w0': jax.random.uniform(ks[10], (L, 2, A_WIDTH), f32, -6.0, 1.0),
        'w2': nrm(ks[11], (L, 2, DECAY_RANK, A_WIDTH), 0.1 * DECAY_RANK ** -0.5),
        'a0': nrm(ks[12], (L, 2, A_WIDTH), 0.5),
        'a2': nrm(ks[13], (L, 2, ICLR_RANK, A_WIDTH), 0.5 * ICLR_RANK ** -0.5),
        'k_k': 0.85 + nrm(ks[14], (L, A_WIDTH), 0.05),
        'k_a': 1.0 + nrm(ks[15], (L, A_WIDTH), 0.05),
        'r_k': nrm(ks[16], (L, A_HEADS, A_HEAD_DIM), 0.1),
        'lnx_w': 1.0 + nrm(ks[17], (L, A_WIDTH), 0.02),
        'lnx_b': nrm(ks[18], (L, A_WIDTH), 0.02),
        'q_gain': 1.0 + nrm(ks[19], (L, B_HEAD_DIM), 0.02),
        'k_gain': 1.0 + nrm(ks[20], (L, B_HEAD_DIM), 0.02),
        'w_out_a': nrm(ks[21], (L, A_WIDTH, D_MODEL), A_WIDTH ** -0.5),
        'w_out_b': nrm(ks[22], (L, B_WIDTH, D_MODEL), B_WIDTH ** -0.5),
        'w_out': nrm(ks[23], (L, D_MODEL, D_MODEL), D_MODEL ** -0.5),
    }


def reference(x_prompt, x_sample, c_prompt, c_sample, rel_bias, norm_w, w_ada, b_ada, w_in,
              shift_mu, w0, w2, a0, a2, k_k, k_a, r_k, lnx_w, lnx_b, q_gain, k_gain,
              w_out_a, w_out_b, w_out):
    def trunk(x, c):
        for l in range(DEPTH):
            x = encoder_layer(x, c, rel_bias, norm_w[l], w_ada[l], b_ada[l], w_in[l], shift_mu[l],
                              w0[l], w2[l], a0[l], a2[l], k_k[l], k_a[l], r_k[l], lnx_w[l], lnx_b[l],
                              q_gain[l], k_gain[l], w_out_a[l], w_out_b[l], w_out[l])
        return x

    y_prompt = trunk(x_prompt, c_prompt)
    y_sample = trunk(x_sample, c_sample)
    return (y_prompt, y_sample)
```

```python
import functools
import math

import numpy as np
import jax
import jax.numpy as jnp
from jax import lax
from jax.experimental import pallas as pl
from jax.experimental.pallas import tpu as pltpu

F32 = jnp.float32
BF16 = jnp.bfloat16
HIGHEST = lax.Precision.HIGHEST

D_MODEL = 1024
A_HEADS = 8
A_HEAD_DIM = 64
A_WIDTH = A_HEADS * A_HEAD_DIM
LOW_RANK = 64
A_COLS = 4 * A_WIDTH + 4 * LOW_RANK
GN_EPS = 64e-5
B_GROUPS = ((128, 1), (512, 4), (2048, 16))
N_GROUPS = len(B_GROUPS)
B_HPG = 4
B_HEAD_DIM = 128
B_HEADS = N_GROUPS * B_HPG
B_QKV = B_HEADS * B_HEAD_DIM
B_WIDTH = B_HPG * B_HEAD_DIM
Q_BLOCK = 128
HALF = 64
N_BUCKETS = 32
MAX_EXACT = 8
MAX_DISTANCE = 1024
PG_COLS = B_WIDTH + 2 * D_MODEL
IN_COLS = A_COLS + 3 * B_QKV + PG_COLS
RMS_EPS = 1e-6
NEG = -1e30

LANES = 128
SUBLANES = 8
MXU_DIM = 256
VMEM_LIMIT_BYTES = 60000 * 1024

PROJ_ROWS = 256
CHUNK = 64
HEADS_PER_STACK = MXU_DIM // CHUNK
STACK_LANES = HEADS_PER_STACK * A_HEAD_DIM
N_STACKS = A_HEADS // HEADS_PER_STACK
MERGE_ROWS = 512

_MXU_DTYPE = BF16


def _dot(a, b, precision=None):
    return jnp.dot(a, b, preferred_element_type=F32, precision=precision)


def _dot_nt(a, b, precision=None):
    return lax.dot_general(a, b, (((1,), (1,)), ((), ())), preferred_element_type=F32, precision=precision)


def _dot_tn(a, b, precision=None):
    return lax.dot_general(a, b, (((0,), (0,)), ((), ())), preferred_element_type=F32, precision=precision)


def _sigmoid(z):
    return 1.0 / (1.0 + jnp.exp(-z))


def _silu(z):
    return z * _sigmoid(z)


def _modulation_kernel(c_ref, w_ref, b_ref, o_ref):
    o_ref[...] = _dot(_silu(c_ref[...]), w_ref[...], HIGHEST) + b_ref[...]


def _modulation(c, w_ada, b_ada):
    bn = c.shape[0]
    return pl.pallas_call(
        _modulation_kernel,
        out_shape=jax.ShapeDtypeStruct((bn, 3 * D_MODEL), F32),
        compiler_params=pltpu.CompilerParams(vmem_limit_bytes=VMEM_LIMIT_BYTES),
        name="modulation",
    )(c, w_ada, b_ada.reshape(1, 3 * D_MODEL))


def _proj_kernel(x_ref, mod_ref, nw_ref, w_ref, qg_ref, kg_ref, pa_ref, q_ref, k_ref, v_ref, pg_ref):
    x = x_ref[0]
    ms = jnp.mean(x * x, axis=-1, keepdims=True)
    h = x * lax.rsqrt(ms + RMS_EPS) * nw_ref[...] * (1.0 + mod_ref[0, 1:2, :]) + mod_ref[0, 0:1, :]
    hb = h.astype(_MXU_DTYPE)
    step = 2 * MXU_DIM

    def cols(c0, width):
        return _dot(hb, w_ref[:, c0:c0 + width])

    for c0 in range(0, A_COLS, step):
        width = min(step, A_COLS - c0)
        pa_ref[0, :, c0:c0 + width] = cols(c0, width)

    def head_norm(z, gain):
        return z * lax.rsqrt(jnp.mean(z * z, axis=-1, keepdims=True) + RMS_EPS) * gain

    heads_per_step = step // B_HEAD_DIM
    for c0 in range(0, B_QKV, step):
        qc = cols(A_COLS + c0, step)
        kc = cols(A_COLS + B_QKV + c0, step)
        vc = cols(A_COLS + 2 * B_QKV + c0, step)
        for j in range(heads_per_step):
            hd = c0 // B_HEAD_DIM + j
            sl = slice(j * B_HEAD_DIM, (j + 1) * B_HEAD_DIM)
            q_ref[0, hd] = head_norm(qc[:, sl], qg_ref[...])
            k_ref[0, hd] = head_norm(kc[:, sl], kg_ref[...])
            v_ref[0, hd] = vc[:, sl]

    pg0 = A_COLS + 3 * B_QKV
    for c0 in range(0, PG_COLS, step):
        pg_ref[0, :, c0:c0 + step] = cols(pg0 + c0, step)


def _projection(x, mod3, norm_w, w_in_b, q_gain, k_gain):
    bn, t, _ = x.shape
    tm = PROJ_ROWS
    grid = (bn, t // tm)
    const2 = lambda b, i: (0, 0)
    return pl.pallas_call(
        _proj_kernel,
        grid=grid,
        in_specs=[
            pl.BlockSpec((1, tm, D_MODEL), lambda b, i: (b, i, 0)),
            pl.BlockSpec((1, 3, D_MODEL), lambda b, i: (b, 0, 0)),
            pl.BlockSpec((1, D_MODEL), const2),
            pl.BlockSpec((D_MODEL, IN_COLS), const2, pipeline_mode=pl.Buffered(1)),
            pl.BlockSpec((1, B_HEAD_DIM), const2),
            pl.BlockSpec((1, B_HEAD_DIM), const2),
        ],
        out_specs=[
            pl.BlockSpec((1, tm, A_COLS), lambda b, i: (b, i, 0)),
            pl.BlockSpec((1, B_HEADS, tm, B_HEAD_DIM), lambda b, i: (b, 0, i, 0)),
            pl.BlockSpec((1, B_HEADS, tm, B_HEAD_DIM), lambda b, i: (b, 0, i, 0)),
            pl.BlockSpec((1, B_HEADS, tm, B_HEAD_DIM), lambda b, i: (b, 0, i, 0)),
            pl.BlockSpec((1, tm, PG_COLS), lambda b, i: (b, i, 0)),
        ],
        out_shape=[
            jax.ShapeDtypeStruct((bn, t, A_COLS), F32),
            jax.ShapeDtypeStruct((bn, B_HEADS, t, B_HEAD_DIM), F32),
            jax.ShapeDtypeStruct((bn, B_HEADS, t, B_HEAD_DIM), F32),
            jax.ShapeDtypeStruct((bn, B_HEADS, t, B_HEAD_DIM), F32),
            jax.ShapeDtypeStruct((bn, t, PG_COLS), F32),
        ],
        compiler_params=pltpu.CompilerParams(
            dimension_semantics=("parallel", "parallel"), vmem_limit_bytes=VMEM_LIMIT_BYTES),
        name="projection",
    )(x, mod3, norm_w.reshape(1, D_MODEL), w_in_b, q_gain.reshape(1, B_HEAD_DIM), k_gain.reshape(1, B_HEAD_DIM))


def _rwkv_kernel(pa_ref, prev_ref, next_ref, mu_ref, w0_ref, w2_ref, a0_ref, a2_ref, kk_ref, ka_ref, rk_ref,
                 y_ref, z_ref, h_scr):
    d = pl.program_id(1)
    i = pl.program_id(2)
    n = pl.num_programs(2)
    ci = jnp.where(d == 0, i, n - 1 - i)
    fwd = d == 0
    sgn = 1 - 2 * d
    c = CHUNK
    hp = HIGHEST

    @pl.when(i == 0)
    def _():
        h_scr[...] = jnp.zeros_like(h_scr)

    pa = pa_ref[0]
    prev_row = jnp.where(ci > 0, prev_ref[0, SUBLANES - 1:SUBLANES, :], 0.0)
    next_row = jnp.where(ci < n - 1, next_ref[0, 0:1, :], 0.0)
    row = lax.broadcasted_iota(jnp.int32, (c, 1), 0)
    prev = jnp.where(row == 0, prev_row, pltpu.roll(pa, 1, 0))
    nxt = jnp.where(row == c - 1, next_row, pltpu.roll(pa, c - 1, 0))
    xs = pa + mu_ref[0:1, :] * (prev - pa) + mu_ref[1:2, :] * (nxt - pa)

    aw = A_WIDTH
    r = xs[:, 0:aw]
    k = xs[:, aw:2 * aw]
    v = xs[:, 2 * aw:3 * aw]
    g = xs[:, 3 * aw:4 * aw]
    wlo = xs[:, 4 * aw:4 * aw + 2 * LOW_RANK]
    alo = xs[:, 4 * aw + 2 * LOW_RANK:4 * aw + 4 * LOW_RANK]

    u = w0_ref[0] + _dot(jnp.tanh(wlo), w2_ref[0], hp)
    logw = -math.exp(-0.5) * _sigmoid(u)
    a_f = _sigmoid(a0_ref[0] + _dot(alo, a2_ref[0], hp))
    a_b = _sigmoid(a0_ref[1] + _dot(alo, a2_ref[1], hp))
    a_d = jnp.where(fwd, a_f, a_b)

    li = lax.broadcasted_iota(jnp.int32, (aw, aw), 0) // A_HEAD_DIM
    lj = lax.broadcasted_iota(jnp.int32, (aw, aw), 1) // A_HEAD_DIM
    head_ones = (li == lj).astype(F32)
    kk = k * kk_ref[...]
    kk_norm = jnp.sqrt(_dot(kk * kk, head_ones, hp))
    kk = kk / jnp.maximum(kk_norm, 1e-12)
    ka = ka_ref[...]
    k_d = k * (1.0 + (a_d - 1.0) * ka)
    b_d = kk * a_d
    a_vec = -kk
    k_both = k * (2.0 + (a_f + a_b - 2.0) * ka)
    bonus = _dot(r * k_both * rk_ref[...], head_ones, hp) * v
    sg = _silu(g)
    z_ref[0, 0] = jnp.where(fwd, sg, bonus * sg)

    ti = lax.broadcasted_iota(jnp.int32, (c, c), 0)
    tj = lax.broadcasted_iota(jnp.int32, (c, c), 1)
    tri = ((ti - tj) * sgn >= 0).astype(F32)
    cl = _dot(tri, logw, hp)
    cl_end = jnp.where(fwd, cl[c - 1:c, :], cl[0:1, :])
    e_in = jnp.exp(cl)
    e_out = jnp.exp(-cl)
    e_tail = jnp.exp(cl_end - cl)
    p_end = jnp.exp(cl_end)
    a_t = a_vec * jnp.exp(cl - logw)
    r_t = r * e_in
    b_t = b_d * e_out
    k_t = k_d * e_out
    b_tail = b_d * e_tail
    k_tail = k_d * e_tail

    m = MXU_DIM
    si = lax.broadcasted_iota(jnp.int32, (m, m), 0)
    sj = lax.broadcasted_iota(jnp.int32, (m, m), 1)
    ri = si % c
    rj = sj % c
    strict = (ri - rj) * sgn > 0
    incl = (ri - rj) * sgn >= 0
    eye = (si == sj).astype(F32)
    lane_head = lax.broadcasted_iota(jnp.int32, (1, STACK_LANES), 1) // A_HEAD_DIM
    head_mask = [(lane_head == hh).astype(F32) for hh in range(HEADS_PER_STACK)]

    for s in range(N_STACKS):
        sl = slice(s * STACK_LANES, (s + 1) * STACK_LANES)

        def stack(z):
            zs = z[:, sl]
            return jnp.concatenate([zs * head_mask[hh] for hh in range(HEADS_PER_STACK)], axis=0)

        a_s, r_s, b_s, k_s, v_s = stack(a_t), stack(r_t), stack(b_t), stack(k_t), stack(v)
        bt_s, kt_s = stack(b_tail), stack(k_tail)
        l_ab = jnp.where(strict, _dot_nt(a_s, b_s, hp), 0.0)
        l_ak = jnp.where(strict, _dot_nt(a_s, k_s, hp), 0.0)
        m_rb = jnp.where(incl, _dot_nt(r_s, b_s, hp), 0.0)
        m_rk = jnp.where(incl, _dot_nt(r_s, k_s, hp), 0.0)
        inv = eye + l_ab
        pw = l_ab
        for _ in range(int(math.log2(c)) - 1):
            pw = _dot(pw, pw, hp)
            inv = inv + _dot(pw, inv, hp)
        a_hat = _dot(inv, a_s, hp)
        u0 = _dot(inv, _dot(l_ak, v_s, hp), hp)
        r_hat = r_s + _dot(m_rb, a_hat, hp)
        y0 = _dot(m_rb, u0, hp) + _dot(m_rk, v_s, hp)
        gmat = eye * p_end[:, sl] + _dot_tn(bt_s, a_hat, hp)
        h_add = _dot_tn(bt_s, u0, hp) + _dot_tn(kt_s, v_s, hp)
        h0 = h_scr[s]
        y_s = _dot(r_hat, h0, hp) + y0
        h_scr[s] = _dot(gmat, h0, hp) + h_add
        y = y_s[0:c]
        for hh in range(1, HEADS_PER_STACK):
            y = y + y_s[hh * c:(hh + 1) * c]
        y_ref[0, 0, :, sl] = y


def _rwkv(pa, shift_mu, w0, w2p, a0, a2p, k_k, k_a, r_k):
    bn, t, _ = pa.shape
    c = CHUNK
    n = t // c
    rows8 = c // SUBLANES

    def chunk_of(d, i):
        return jnp.where(d == 0, i, n - 1 - i)

    def main_map(b, d, i):
        return (b, chunk_of(d, i), 0)

    def prev_map(b, d, i):
        return (b, jnp.maximum(chunk_of(d, i) * rows8 - 1, 0), 0)

    def next_map(b, d, i):
        return (b, jnp.minimum((chunk_of(d, i) + 1) * rows8, t // SUBLANES - 1), 0)

    def out_map(b, d, i):
        return (b, d, chunk_of(d, i), 0)

    const2 = lambda b, d, i: (0, 0)
    const3 = lambda b, d, i: (0, 0, 0)
    by_dir = lambda b, d, i: (d, 0, 0)
    return pl.pallas_call(
        _rwkv_kernel,
        grid=(bn, 2, n),
        in_specs=[
            pl.BlockSpec((1, c, A_COLS), main_map),
            pl.BlockSpec((1, SUBLANES, A_COLS), prev_map),
            pl.BlockSpec((1, SUBLANES, A_COLS), next_map),
            pl.BlockSpec((2, A_COLS), const2),
            pl.BlockSpec((1, 1, A_WIDTH), by_dir),
            pl.BlockSpec((1, 2 * LOW_RANK, A_WIDTH), by_dir),
            pl.BlockSpec((2, 1, A_WIDTH), const3),
            pl.BlockSpec((2, 2 * LOW_RANK, A_WIDTH), const3),
            pl.BlockSpec((1, A_WIDTH), const2),
            pl.BlockSpec((1, A_WIDTH), const2),
            pl.BlockSpec((1, A_WIDTH), const2),
        ],
        out_specs=[
            pl.BlockSpec((1, 1, c, A_WIDTH), out_map),
            pl.BlockSpec((1, 1, c, A_WIDTH), out_map),
        ],
        out_shape=[
            jax.ShapeDtypeStruct((bn, 2, t, A_WIDTH), F32),
            jax.ShapeDtypeStruct((bn, 2, t, A_WIDTH), F32),
        ],
        scratch_shapes=[pltpu.VMEM((N_STACKS, MXU_DIM, MXU_DIM), F32)],
        compiler_params=pltpu.CompilerParams(
            dimension_semantics=("parallel", "arbitrary", "arbitrary"), vmem_limit_bytes=VMEM_LIMIT_BYTES),
        name="rwkv",
    )(pa, pa, pa, shift_mu, w0.reshape(2, 1, A_WIDTH), w2p, a0.reshape(2, 1, A_WIDTH), a2p,
      k_k.reshape(1, A_WIDTH), k_a.reshape(1, A_WIDTH), r_k.reshape(1, A_WIDTH))


def _t5_bucket_np(rel):
    half_b = N_BUCKETS // 2
    ret = np.where(rel > 0, half_b, 0)
    n = np.abs(rel)
    large = MAX_EXACT + (np.log(np.maximum(n, 1).astype(np.float32) / MAX_EXACT)
                         / math.log(MAX_DISTANCE / MAX_EXACT) * (half_b - MAX_EXACT)).astype(np.int32)
    large = np.minimum(large, half_b - 1)
    return ret + np.where(n < MAX_EXACT, n, large)


def _band_buckets():
    i = np.arange(Q_BLOCK)[:, None]
    j = np.arange(2 * Q_BLOCK)[None, :]
    rel = j - HALF - i
    out = []
    for _, dil in B_GROUPS:
        bkt = _t5_bucket_np(rel * dil)
        out.append(np.where(np.abs(rel) <= HALF, bkt, -1))
    return np.stack(out).astype(np.int32)


def _attn_kernel(tbl_ref, bkt_ref, q_ref, k_ref, v_ref, o_ref, m_scr, l_scr, bias_scr, *, t):
    h4 = pl.program_id(1)
    gi = pl.program_id(2)
    head = gi * B_HPG + h4
    qb = Q_BLOCK
    scale = B_HEAD_DIM ** -0.5

    @pl.when(gi == 0)
    def _():
        m_scr[...] = jnp.full_like(m_scr, NEG)
        l_scr[...] = jnp.zeros_like(l_scr)
        o_ref[...] = jnp.zeros_like(o_ref)

    bkt = bkt_ref[0]
    bias = jnp.full((qb, 2 * qb), NEG, F32)
    for bb in range(N_BUCKETS):
        bias = jnp.where(bkt == bb, tbl_ref[bb, head], bias)
    bias_scr[...] = bias
    col = lax.broadcasted_iota(jnp.int32, (1, 2 * qb), 1)

    def run(dil):
        sub_len = t // dil
        nb = sub_len // qb

        def body(it, carry):
            rho = it // nb
            bq = it % nb
            q_rows = pl.ds(rho + dil * qb * bq, qb, stride=dil)
            s0 = jnp.maximum(qb * bq - HALF, 0)
            s3 = jnp.minimum(qb * bq + qb, sub_len - HALF)
            rows0 = pl.ds(rho + dil * s0, HALF, stride=dil)
            rows3 = pl.ds(rho + dil * s3, HALF, stride=dil)

            def window(ref):
                return jnp.concatenate([ref[0, 0, rows0, :], ref[0, 0, q_rows, :], ref[0, 0, rows3, :]], axis=0)

            q = q_ref[0, 0, q_rows, :].astype(_MXU_DTYPE)
            kw = window(k_ref).astype(_MXU_DTYPE)
            vw = window(v_ref).astype(_MXU_DTYPE)
            bias_b = bias_scr[...]
            pos = qb * bq - HALF + col
            ok = (bkt_ref[0] >= 0) & (pos >= 0) & (pos < sub_len)
            s = jnp.where(ok, _dot_nt(q, kw) * scale + bias_b, NEG)
            m_old = m_scr[q_rows, :]
            l_old = l_scr[q_rows, :]
            acc_old = o_ref[0, q_rows, :]
            m_new = jnp.maximum(m_old, jnp.max(s, axis=-1, keepdims=True))
            alpha = jnp.exp(m_old - m_new)
            p = jnp.exp(s - m_new[:, 0:1])
            m_scr[q_rows, :] = m_new
            l_scr[q_rows, :] = alpha * l_old + jnp.sum(p, axis=-1, keepdims=True)
            o_ref[0, q_rows, :] = alpha * acc_old + _dot(p.astype(_MXU_DTYPE), vw)
            return carry

        lax.fori_loop(0, dil * nb, body, 0)

    for gidx, (_, dil) in enumerate(B_GROUPS):
        pl.when(gi == gidx)(functools.partial(run, dil))

    @pl.when(gi == N_GROUPS - 1)
    def _():
        o_ref[0] = o_ref[0] / l_scr[...]


def _attention(rel_bias, buckets, q, k, v):
    bn, _, t, _ = q.shape
    qkv_spec = pl.BlockSpec((1, 1, t, B_HEAD_DIM), lambda b, h, g: (b, g * B_HPG + h, 0, 0))
    return pl.pallas_call(
        functools.partial(_attn_kernel, t=t),
        grid=(bn, B_HPG, N_GROUPS),
        in_specs=[
            pl.BlockSpec(memory_space=pltpu.SMEM),
            pl.BlockSpec((1, Q_BLOCK, 2 * Q_BLOCK), lambda b, h, g: (g, 0, 0)),
            qkv_spec, qkv_spec, qkv_spec,
        ],
        out_specs=pl.BlockSpec((1, t, B_HEAD_DIM), lambda b, h, g: (b, 0, h)),
        out_shape=jax.ShapeDtypeStruct((bn, t, B_WIDTH), F32),
        scratch_shapes=[
            pltpu.VMEM((t, B_HEAD_DIM), F32),
            pltpu.VMEM((t, B_HEAD_DIM), F32),
            pltpu.VMEM((Q_BLOCK, 2 * Q_BLOCK), F32),
        ],
        compiler_params=pltpu.CompilerParams(
            dimension_semantics=("parallel", "parallel", "arbitrary"), vmem_limit_bytes=VMEM_LIMIT_BYTES),
        name="attention",
    )(rel_bias, buckets, q, k, v)


def _merge_kernel(x_ref, mod_ref, y_ref, z_ref, o_ref, pg_ref, lw_ref, lb_ref, wa_ref, wb_ref, wo_ref, out_ref):
    aw = A_WIDTH
    y = y_ref[0, 0] + y_ref[0, 1]
    li = lax.broadcasted_iota(jnp.int32, (aw, aw), 0) // A_HEAD_DIM
    lj = lax.broadcasted_iota(jnp.int32, (aw, aw), 1) // A_HEAD_DIM
    head_mean = (li == lj).astype(F32) * (1.0 / A_HEAD_DIM)
    mean = _dot(y, head_mean, HIGHEST)
    yc = y - mean
    var = _dot(yc * yc, head_mean, HIGHEST)
    yn = yc * lax.rsqrt(var + GN_EPS) * lw_ref[...] + lb_ref[...]
    ya = yn * z_ref[0, 0] + z_ref[0, 1]
    pg = pg_ref[0]
    yb = o_ref[0] * _silu(pg[:, 0:B_WIDTH])
    gate_a = _sigmoid(pg[:, B_WIDTH:B_WIDTH + D_MODEL])
    gate_b = _sigmoid(pg[:, B_WIDTH + D_MODEL:])
    merged = (gate_a * _dot(ya.astype(_MXU_DTYPE), wa_ref[...])
              + gate_b * _dot(yb.astype(_MXU_DTYPE), wb_ref[...]))
    out_ref[0] = x_ref[0] + mod_ref[0, 2:3, :] * _dot(merged.astype(_MXU_DTYPE), wo_ref[...])


def _merge(x, mod3, y, z, o, pg, lnx_w, lnx_b, wa_b, wb_b, wo_b):
    bn, t, _ = x.shape
    tm = MERGE_ROWS
    const2 = lambda b, i: (0, 0)
    return pl.pallas_call(
        _merge_kernel,
        grid=(bn, t // tm),
        in_specs=[
            pl.BlockSpec((1, tm, D_MODEL), lambda b, i: (b, i, 0)),
            pl.BlockSpec((1, 3, D_MODEL), lambda b, i: (b, 0, 0)),
            pl.BlockSpec((1, 2, tm, A_WIDTH), lambda b, i: (b, 0, i, 0)),
            pl.BlockSpec((1, 2, tm, A_WIDTH), lambda b, i: (b, 0, i, 0)),
            pl.BlockSpec((1, tm, B_WIDTH), lambda b, i: (b, i, 0)),
            pl.BlockSpec((1, tm, PG_COLS), lambda b, i: (b, i, 0)),
            pl.BlockSpec((1, A_WIDTH), const2),
            pl.BlockSpec((1, A_WIDTH), const2),
            pl.BlockSpec((A_WIDTH, D_MODEL), const2),
            pl.BlockSpec((B_WIDTH, D_MODEL), const2),
            pl.BlockSpec((D_MODEL, D_MODEL), const2),
        ],
        out_specs=pl.BlockSpec((1, tm, D_MODEL), lambda b, i: (b, i, 0)),
        out_shape=jax.ShapeDtypeStruct((bn, t, D_MODEL), x.dtype),
        compiler_params=pltpu.CompilerParams(
            dimension_semantics=("parallel", "parallel"), vmem_limit_bytes=VMEM_LIMIT_BYTES),
        name="merge",
    )(x, mod3, y, z, o, pg, lnx_w.reshape(1, A_WIDTH), lnx_b.reshape(1, A_WIDTH), wa_b, wb_b, wo_b)


def _pad_low_rank(w):
    z = jnp.zeros((LOW_RANK, A_WIDTH), w.dtype)
    return jnp.stack([jnp.concatenate([w[0], z], axis=0), jnp.concatenate([z, w[1]], axis=0)])


def kernel(x_prompt, x_sample, c_prompt, c_sample, rel_bias, norm_w, w_ada, b_ada, w_in, shift_mu, w0, w2, a0, a2,
           k_k, k_a, r_k, lnx_w, lnx_b, q_gain, k_gain, w_out_a, w_out_b, w_out):
    depth = norm_w.shape[0]
    buckets = jnp.asarray(_band_buckets())

    def trunk(x, c):
        bn, t, _ = x.shape
        assert t % (Q_BLOCK * B_GROUPS[-1][1]) == 0 and t % MERGE_ROWS == 0
        for l in range(depth):
            mod3 = _modulation(c, w_ada[l], b_ada[l]).reshape(bn, 3, D_MODEL)
            pa, q, k, v, pg = _projection(x, mod3, norm_w[l], w_in[l].astype(_MXU_DTYPE), q_gain[l], k_gain[l])
            y, z = _rwkv(pa, shift_mu[l], w0[l], _pad_low_rank(w2[l]), a0[l], _pad_low_rank(a2[l]),
                         k_k[l], k_a[l], r_k[l].reshape(A_WIDTH))
            o = _attention(rel_bias, buckets, q, k, v)
            x = _merge(x, mod3, y, z, o, pg, lnx_w[l], lnx_b[l], w_out_a[l].astype(_MXU_DTYPE),
                       w_out_b[l].astype(_MXU_DTYPE), w_out[l].astype(_MXU_DTYPE))
        return x

    return (trunk(x_prompt, c_prompt), trunk(x_sample, c_sample))
```

```python
import functools
import math

import numpy as np
import jax
import jax.numpy as jnp
from jax import lax
from jax.experimental import pallas as pl
from jax.experimental.pallas import tpu as pltpu

F32 = jnp.float32
BF16 = jnp.bfloat16
HIGHEST = lax.Precision.HIGHEST

D_MODEL = 1024
A_HEADS = 8
A_HEAD_DIM = 64
A_WIDTH = A_HEADS * A_HEAD_DIM
LOW_RANK = 64
A_COLS = 4 * A_WIDTH + 4 * LOW_RANK
GN_EPS = 64e-5
B_GROUPS = ((128, 1), (512, 4), (2048, 16))
N_GROUPS = len(B_GROUPS)
B_HPG = 4
B_HEAD_DIM = 128
B_HEADS = N_GROUPS * B_HPG
B_QKV = B_HEADS * B_HEAD_DIM
B_WIDTH = B_HPG * B_HEAD_DIM
Q_BLOCK = 128
HALF = 64
N_BUCKETS = 32
MAX_EXACT = 8
MAX_DISTANCE = 1024
PG_COLS = B_WIDTH + 2 * D_MODEL
IN_COLS = A_COLS + 3 * B_QKV + PG_COLS
RMS_EPS = 1e-6
NEG = -1e30

LANES = 128
SUBLANES = 8
MXU_DIM = 256
VMEM_LIMIT_BYTES = 60000 * 1024

PROJ_ROWS = 256
CHUNK = 64
PREP_ROWS = 256
SCAN_SUB = 4
HEADS_PER_STACK = MXU_DIM // CHUNK
STACK_LANES = HEADS_PER_STACK * A_HEAD_DIM
N_STACKS = A_HEADS // HEADS_PER_STACK
MERGE_ROWS = 512

_MXU_DTYPE = BF16


def _dot(a, b, precision=None):
    return jnp.dot(a, b, preferred_element_type=F32, precision=precision)


def _dot_nt(a, b, precision=None):
    return lax.dot_general(a, b, (((1,), (1,)), ((), ())), preferred_element_type=F32, precision=precision)


def _dot_tn(a, b, precision=None):
    return lax.dot_general(a, b, (((0,), (0,)), ((), ())), preferred_element_type=F32, precision=precision)


def _sigmoid(z):
    return 1.0 / (1.0 + jnp.exp(-z))


def _silu(z):
    return z * _sigmoid(z)


def _modulation_kernel(c_ref, w_ref, b_ref, o_ref):
    o_ref[...] = _dot(_silu(c_ref[...]), w_ref[...], HIGHEST) + b_ref[...]


def _modulation(c, w_ada, b_ada):
    bn = c.shape[0]
    return pl.pallas_call(
        _modulation_kernel,
        out_shape=jax.ShapeDtypeStruct((bn, 3 * D_MODEL), F32),
        compiler_params=pltpu.CompilerParams(vmem_limit_bytes=VMEM_LIMIT_BYTES),
        name="modulation",
    )(c, w_ada, b_ada.reshape(1, 3 * D_MODEL))


def _proj_kernel(x_ref, mod_ref, nw_ref, w_ref, qg_ref, kg_ref, pa_ref, q_ref, k_ref, v_ref, pg_ref):
    x = x_ref[0]
    ms = jnp.mean(x * x, axis=-1, keepdims=True)
    h = x * lax.rsqrt(ms + RMS_EPS) * nw_ref[...] * (1.0 + mod_ref[0, 1:2, :]) + mod_ref[0, 0:1, :]
    hb = h.astype(_MXU_DTYPE)
    step = 2 * MXU_DIM

    def cols(c0, width):
        return _dot(hb, w_ref[:, c0:c0 + width])

    for c0 in range(0, A_COLS, step):
        width = min(step, A_COLS - c0)
        pa_ref[0, :, c0:c0 + width] = cols(c0, width)

    def head_norm(z, gain):
        return z * lax.rsqrt(jnp.mean(z * z, axis=-1, keepdims=True) + RMS_EPS) * gain

    heads_per_step = step // B_HEAD_DIM
    for c0 in range(0, B_QKV, step):
        qc = cols(A_COLS + c0, step)
        kc = cols(A_COLS + B_QKV + c0, step)
        vc = cols(A_COLS + 2 * B_QKV + c0, step)
        for j in range(heads_per_step):
            hd = c0 // B_HEAD_DIM + j
            sl = slice(j * B_HEAD_DIM, (j + 1) * B_HEAD_DIM)
            q_ref[0, hd] = head_norm(qc[:, sl], qg_ref[...])
            k_ref[0, hd] = head_norm(kc[:, sl], kg_ref[...])
            v_ref[0, hd] = vc[:, sl]

    pg0 = A_COLS + 3 * B_QKV
    for c0 in range(0, PG_COLS, step):
        pg_ref[0, :, c0:c0 + step] = cols(pg0 + c0, step)


def _projection(x, mod3, norm_w, w_in_b, q_gain, k_gain):
    bn, t, _ = x.shape
    tm = PROJ_ROWS
    grid = (bn, t // tm)
    const2 = lambda b, i: (0, 0)
    return pl.pallas_call(
        _proj_kernel,
        grid=grid,
        in_specs=[
            pl.BlockSpec((1, tm, D_MODEL), lambda b, i: (b, i, 0)),
            pl.BlockSpec((1, 3, D_MODEL), lambda b, i: (b, 0, 0)),
            pl.BlockSpec((1, D_MODEL), const2),
            pl.BlockSpec((D_MODEL, IN_COLS), const2, pipeline_mode=pl.Buffered(1)),
            pl.BlockSpec((1, B_HEAD_DIM), const2),
            pl.BlockSpec((1, B_HEAD_DIM), const2),
        ],
        out_specs=[
            pl.BlockSpec((1, tm, A_COLS), lambda b, i: (b, i, 0)),
            pl.BlockSpec((1, B_HEADS, tm, B_HEAD_DIM), lambda b, i: (b, 0, i, 0)),
            pl.BlockSpec((1, B_HEADS, tm, B_HEAD_DIM), lambda b, i: (b, 0, i, 0)),
            pl.BlockSpec((1, B_HEADS, tm, B_HEAD_DIM), lambda b, i: (b, 0, i, 0)),
            pl.BlockSpec((1, tm, PG_COLS), lambda b, i: (b, i, 0)),
        ],
        out_shape=[
            jax.ShapeDtypeStruct((bn, t, A_COLS), F32),
            jax.ShapeDtypeStruct((bn, B_HEADS, t, B_HEAD_DIM), F32),
            jax.ShapeDtypeStruct((bn, B_HEADS, t, B_HEAD_DIM), F32),
            jax.ShapeDtypeStruct((bn, B_HEADS, t, B_HEAD_DIM), F32),
            jax.ShapeDtypeStruct((bn, t, PG_COLS), F32),
        ],
        compiler_params=pltpu.CompilerParams(
            dimension_semantics=("parallel", "parallel"), vmem_limit_bytes=VMEM_LIMIT_BYTES),
        name="projection",
    )(x, mod3, norm_w.reshape(1, D_MODEL), w_in_b, q_gain.reshape(1, B_HEAD_DIM), k_gain.reshape(1, B_HEAD_DIM))


def _split3(z):
    hi = z.astype(BF16)
    r1 = z - hi.astype(F32)
    mid = r1.astype(BF16)
    lo = (r1 - mid.astype(F32)).astype(BF16)
    return hi, mid, lo


def _dot_exact_lhs(lhs_b, z):
    hi, mid, lo = _split3(z)
    return _dot(lhs_b, hi) + (_dot(lhs_b, mid) + _dot(lhs_b, lo))


def _dot_exact_rhs(z, rhs_b):
    hi, mid, lo = _split3(z)
    return _dot(hi, rhs_b) + (_dot(mid, rhs_b) + _dot(lo, rhs_b))


_MASK_STRICT, _MASK_INCL, _MASK_EYE, _MASK_LEVEL0 = 0, 1, 2, 3
_N_LEVELS = int(math.log2(CHUNK))
_N_MASKS = _MASK_LEVEL0 + _N_LEVELS
_OP_A, _OP_R, _OP_B, _OP_K, _OP_BTAIL, _OP_KTAIL = range(6)
_N_OPS = 6


def _scan_masks():
    idx = np.arange(MXU_DIM)
    out = []
    for d in range(2):
        p = idx % CHUNK if d == 0 else CHUNK - 1 - idx % CHUNK
        pi, pj = p[:, None], p[None, :]
        masks = [pj < pi, pj <= pi, idx[:, None] == idx[None, :]]
        for lv in range(_N_LEVELS):
            sz = 1 << lv
            masks.append(((pi // sz) % 2 == 1) & (pj // sz == pi // sz - 1))
        out.append(np.stack(masks))
    return np.stack(out).astype(np.float32)


def _cumulative_lhs():
    idx = np.arange(PREP_ROWS)
    same = idx[:, None] // CHUNK == idx[None, :] // CHUNK
    fwd = same & (idx[None, :] <= idx[:, None])
    bwd = same & (idx[None, :] >= idx[:, None])
    return np.stack([np.concatenate([fwd, same]), np.concatenate([bwd, same])]).astype(np.float32)


def _prep_kernel(pa_ref, prev_ref, next_ref, mu_ref, w0_ref, w2_ref, a0_ref, a2_ref, kk_ref, ka_ref, rk_ref, ones_ref,
                 cum_ref, ops_ref, v_ref, pend_ref, z_ref):
    i = pl.program_id(1)
    n = pl.num_programs(1)
    c = CHUNK
    rb = PREP_ROWS

    pa = pa_ref[0]
    prev_row = jnp.where(i > 0, prev_ref[0, SUBLANES - 1:SUBLANES, :], 0.0)
    next_row = jnp.where(i < n - 1, next_ref[0, 0:1, :], 0.0)
    row = lax.broadcasted_iota(jnp.int32, (rb, 1), 0)
    prev = jnp.where(row == 0, prev_row, pltpu.roll(pa, 1, 0))
    nxt = jnp.where(row == rb - 1, next_row, pltpu.roll(pa, rb - 1, 0))
    xs = pa + mu_ref[0:1, :] * (prev - pa) + mu_ref[1:2, :] * (nxt - pa)

    aw = A_WIDTH
    r = xs[:, 0:aw]
    k = xs[:, aw:2 * aw]
    v = xs[:, 2 * aw:3 * aw]
    g = xs[:, 3 * aw:4 * aw]
    tw = jnp.tanh(xs[:, 4 * aw:4 * aw + 2 * LOW_RANK]).astype(BF16)
    alo = xs[:, 4 * aw + 2 * LOW_RANK:4 * aw + 4 * LOW_RANK].astype(BF16)

    head_ones = ones_ref[...]
    kk = k * kk_ref[...]
    kk = kk / jnp.maximum(jnp.sqrt(_dot_exact_rhs(kk * kk, head_ones)), 1e-12)
    ka = ka_ref[...]
    a_dir = [_sigmoid(a0_ref[d] + _dot(alo, a2_ref[d])) for d in range(2)]
    k_both = k * (2.0 + (a_dir[0] + a_dir[1] - 2.0) * ka)
    bonus = _dot_exact_rhs(r * k_both * rk_ref[...], head_ones) * v
    sg = _silu(g)
    z_ref[0, 0] = sg
    z_ref[0, 1] = bonus * sg
    v_ref[0] = v.astype(BF16)

    for d in range(2):
        u = w0_ref[d] + _dot(tw, w2_ref[d])
        logw = -math.exp(-0.5) * _sigmoid(u)
        sums = _dot_exact_lhs(cum_ref[d], logw)
        cl = sums[0:rb]
        total = sums[rb:2 * rb]
        e_out = jnp.exp(-cl)
        e_tail = jnp.exp(total - cl)
        k_d = k * (1.0 + (a_dir[d] - 1.0) * ka)
        b_d = kk * a_dir[d]
        ops_ref[0, d, _OP_A] = (-kk * jnp.exp(cl - logw)).astype(BF16)
        ops_ref[0, d, _OP_R] = (r * jnp.exp(cl)).astype(BF16)
        ops_ref[0, d, _OP_B] = (b_d * e_out).astype(BF16)
        ops_ref[0, d, _OP_K] = (k_d * e_out).astype(BF16)
        ops_ref[0, d, _OP_BTAIL] = (b_d * e_tail).astype(BF16)
        ops_ref[0, d, _OP_KTAIL] = (k_d * e_tail).astype(BF16)
        p_end = jnp.exp(total)
        for j in range(rb // c):
            pend_ref[0, d, j] = p_end[j * c:j * c + 1, :]


def _prep(pa, shift_mu, w0, w2p, a0, a2p, k_k, k_a, r_k):
    bn, t, _ = pa.shape
    rb = PREP_ROWS
    n = t // rb
    rows8 = rb // SUBLANES
    head_of = np.arange(A_WIDTH) // A_HEAD_DIM
    head_ones = jnp.asarray(head_of[:, None] == head_of[None, :], dtype=BF16)
    cum_lhs = jnp.asarray(_cumulative_lhs(), dtype=BF16)
    const2 = lambda b, i: (0, 0)
    const3 = lambda b, i: (0, 0, 0)
    return pl.pallas_call(
        _prep_kernel,
        grid=(bn, n),
        in_specs=[
            pl.BlockSpec((1, rb, A_COLS), lambda b, i: (b, i, 0)),
            pl.BlockSpec((1, SUBLANES, A_COLS), lambda b, i: (b, jnp.maximum(i * rows8 - 1, 0), 0)),
            pl.BlockSpec((1, SUBLANES, A_COLS), lambda b, i: (b, jnp.minimum((i + 1) * rows8, t // SUBLANES - 1), 0)),
            pl.BlockSpec((2, A_COLS), const2),
            pl.BlockSpec((2, 1, A_WIDTH), const3),
            pl.BlockSpec((2, 2 * LOW_RANK, A_WIDTH), const3),
            pl.BlockSpec((2, 1, A_WIDTH), const3),
            pl.BlockSpec((2, 2 * LOW_RANK, A_WIDTH), const3),
            pl.BlockSpec((1, A_WIDTH), const2),
            pl.BlockSpec((1, A_WIDTH), const2),
            pl.BlockSpec((1, A_WIDTH), const2),
            pl.BlockSpec((A_WIDTH, A_WIDTH), const2),
            pl.BlockSpec((2, 2 * rb, rb), const3),
        ],
        out_specs=[
            pl.BlockSpec((1, 2, _N_OPS, rb, A_WIDTH), lambda b, i: (b, 0, 0, i, 0)),
            pl.BlockSpec((1, rb, A_WIDTH), lambda b, i: (b, i, 0)),
            pl.BlockSpec((1, 2, rb // CHUNK, 1, A_WIDTH), lambda b, i: (b, 0, i, 0, 0)),
            pl.BlockSpec((1, 2, rb, A_WIDTH), lambda b, i: (b, 0, i, 0)),
        ],
        out_shape=[
            jax.ShapeDtypeStruct((bn, 2, _N_OPS, t, A_WIDTH), BF16),
            jax.ShapeDtypeStruct((bn, t, A_WIDTH), BF16),
            jax.ShapeDtypeStruct((bn, 2, t // CHUNK, 1, A_WIDTH), F32),
            jax.ShapeDtypeStruct((bn, 2, t, A_WIDTH), F32),
        ],
        compiler_params=pltpu.CompilerParams(
            dimension_semantics=("parallel", "parallel"), vmem_limit_bytes=VMEM_LIMIT_BYTES),
        name="rwkv_prep",
    )(pa, pa, pa, shift_mu, w0.reshape(2, 1, A_WIDTH), w2p.astype(BF16), a0.reshape(2, 1, A_WIDTH),
      a2p.astype(BF16), k_k.reshape(1, A_WIDTH), k_a.reshape(1, A_WIDTH), r_k.reshape(1, A_WIDTH), head_ones, cum_lhs)


def _scan_kernel(ops_ref, v_ref, pend_ref, mask_ref, y_ref, h_scr):
    d = pl.program_id(1)
    i = pl.program_id(2)
    fwd = d == 0
    c = CHUNK

    @pl.when(i == 0)
    def _():
        h_scr[...] = jnp.zeros_like(h_scr)

    lane_head = lax.broadcasted_iota(jnp.int32, (1, STACK_LANES), 1) // A_HEAD_DIM
    head_sel = [lane_head == hh for hh in range(HEADS_PER_STACK)]
    strict = mask_ref[0, _MASK_STRICT] != 0.0
    incl = mask_ref[0, _MASK_INCL] != 0.0
    eye = mask_ref[0, _MASK_EYE]

    probs = []
    for j in range(SCAN_SUB):
        jj = jnp.where(fwd, j, SCAN_SUB - 1 - j)
        rows = pl.ds(pl.multiple_of(jj * c, c), c)
        for s in range(N_STACKS):
            probs.append((jj, rows, s, slice(s * STACK_LANES, (s + 1) * STACK_LANES)))
    n_probs = len(probs)

    def stack(ref_slice):
        return jnp.concatenate([jnp.where(head_sel[hh], ref_slice, jnp.zeros_like(ref_slice))
                                for hh in range(HEADS_PER_STACK)], axis=0)

    st = []
    for jj, rows, s, sl in probs:
        st.append({name: stack(ops_ref[0, 0, op, rows, sl]) for name, op in
                   (("a", _OP_A), ("r", _OP_R), ("b", _OP_B), ("k", _OP_K), ("bt", _OP_BTAIL), ("kt", _OP_KTAIL))})
        st[-1]["v"] = stack(v_ref[0, rows, sl])
    l_ab = [jnp.where(strict, _dot_nt(q["a"], q["b"]), 0.0) for q in st]
    l_ak = [jnp.where(strict, _dot_nt(q["a"], q["k"]), 0.0).astype(BF16) for q in st]
    m_rb = [jnp.where(incl, _dot_nt(q["r"], q["b"]), 0.0).astype(BF16) for q in st]
    m_rk = [jnp.where(incl, _dot_nt(q["r"], q["k"]), 0.0).astype(BF16) for q in st]
    inv = [eye + l * mask_ref[0, _MASK_LEVEL0] for l in l_ab]
    for lv in range(1, _N_LEVELS):
        inv_b = [x.astype(BF16) for x in inv]
        lo = [(l * mask_ref[0, _MASK_LEVEL0 + lv]).astype(BF16) for l in l_ab]
        tmp = [_dot(lo[p], inv_b[p]).astype(BF16) for p in range(n_probs)]
        inv = [inv[p] + _dot(inv_b[p], tmp[p]) for p in range(n_probs)]
    inv_b = [x.astype(BF16) for x in inv]
    a_hat = [_dot(inv_b[p], st[p]["a"]).astype(BF16) for p in range(n_probs)]
    w = [_dot(l_ak[p], st[p]["v"]).astype(BF16) for p in range(n_probs)]
    u0 = [_dot(inv_b[p], w[p]).astype(BF16) for p in range(n_probs)]
    r_hat = [(st[p]["r"].astype(F32) + _dot(m_rb[p], a_hat[p])).astype(BF16) for p in range(n_probs)]
    y0 = [_dot(m_rb[p], u0[p]) + _dot(m_rk[p], st[p]["v"]) for p in range(n_probs)]
    gmat = [(eye * pend_ref[0, 0, probs[p][0]][:, probs[p][3]] + _dot_tn(st[p]["bt"], a_hat[p])).astype(BF16)
            for p in range(n_probs)]
    h_add = [_dot_tn(st[p]["bt"], u0[p]) + _dot_tn(st[p]["kt"], st[p]["v"]) for p in range(n_probs)]

    for p, (jj, rows, s, sl) in enumerate(probs):
        h0 = h_scr[s].astype(BF16)
        y_s = _dot(r_hat[p], h0) + y0[p]
        h_scr[s] = _dot(gmat[p], h0) + h_add[p]
        y = y_s[0:c]
        for hh in range(1, HEADS_PER_STACK):
            y = y + y_s[hh * c:(hh + 1) * c]
        y_ref[0, 0, rows, sl] = y


def _scan(ops, v, pend):
    bn, _, _, t, _ = ops.shape
    rb = SCAN_SUB * CHUNK
    n = t // rb
    masks = jnp.asarray(_scan_masks())

    def block_of(d, i):
        return jnp.where(d == 0, i, n - 1 - i)

    return pl.pallas_call(
        _scan_kernel,
        grid=(bn, 2, n),
        in_specs=[
            pl.BlockSpec((1, 1, _N_OPS, rb, A_WIDTH), lambda b, d, i: (b, d, 0, block_of(d, i), 0)),
            pl.BlockSpec((1, rb, A_WIDTH), lambda b, d, i: (b, block_of(d, i), 0)),
            pl.BlockSpec((1, 1, SCAN_SUB, 1, A_WIDTH), lambda b, d, i: (b, d, block_of(d, i), 0, 0)),
            pl.BlockSpec((1, _N_MASKS, MXU_DIM, MXU_DIM), lambda b, d, i: (d, 0, 0, 0)),
        ],
        out_specs=pl.BlockSpec((1, 1, rb, A_WIDTH), lambda b, d, i: (b, d, block_of(d, i), 0)),
        out_shape=jax.ShapeDtypeStruct((bn, 2, t, A_WIDTH), F32),
        scratch_shapes=[pltpu.VMEM((N_STACKS, MXU_DIM, MXU_DIM), F32)],
        compiler_params=pltpu.CompilerParams(
            dimension_semantics=("parallel", "arbitrary", "arbitrary"), vmem_limit_bytes=VMEM_LIMIT_BYTES),
        name="rwkv_scan",
    )(ops, v, pend, masks)


def _rwkv(pa, shift_mu, w0, w2p, a0, a2p, k_k, k_a, r_k):
    ops, v, pend, z = _prep(pa, shift_mu, w0, w2p, a0, a2p, k_k, k_a, r_k)
    return _scan(ops, v, pend), z


def _t5_bucket_np(rel):
    half_b = N_BUCKETS // 2
    ret = np.where(rel > 0, half_b, 0)
    n = np.abs(rel)
    large = MAX_EXACT + (np.log(np.maximum(n, 1).astype(np.float32) / MAX_EXACT)
                         / math.log(MAX_DISTANCE / MAX_EXACT) * (half_b - MAX_EXACT)).astype(np.int32)
    large = np.minimum(large, half_b - 1)
    return ret + np.where(n < MAX_EXACT, n, large)


def _band_buckets():
    i = np.arange(Q_BLOCK)[:, None]
    j = np.arange(2 * Q_BLOCK)[None, :]
    rel = j - HALF - i
    out = []
    for _, dil in B_GROUPS:
        bkt = _t5_bucket_np(rel * dil)
        out.append(np.where(np.abs(rel) <= HALF, bkt, -1))
    return np.stack(out).astype(np.int32)


def _attn_kernel(tbl_ref, bkt_ref, q_ref, k_ref, v_ref, o_ref, m_scr, l_scr, bias_scr, *, t):
    h4 = pl.program_id(1)
    gi = pl.program_id(2)
    head = gi * B_HPG + h4
    qb = Q_BLOCK
    scale = B_HEAD_DIM ** -0.5

    @pl.when(gi == 0)
    def _():
        m_scr[...] = jnp.full_like(m_scr, NEG)
        l_scr[...] = jnp.zeros_like(l_scr)
        o_ref[...] = jnp.zeros_like(o_ref)

    bkt = bkt_ref[0]
    bias = jnp.full((qb, 2 * qb), NEG, F32)
    for bb in range(N_BUCKETS):
        bias = jnp.where(bkt == bb, tbl_ref[bb, head], bias)
    bias_scr[...] = bias
    col = lax.broadcasted_iota(jnp.int32, (1, 2 * qb), 1)

    def run(dil):
        sub_len = t // dil
        nb = sub_len // qb

        def body(it, carry):
            rho = it // nb
            bq = it % nb
            q_rows = pl.ds(rho + dil * qb * bq, qb, stride=dil)
            s0 = jnp.maximum(qb * bq - HALF, 0)
            s3 = jnp.minimum(qb * bq + qb, sub_len - HALF)
            rows0 = pl.ds(rho + dil * s0, HALF, stride=dil)
            rows3 = pl.ds(rho + dil * s3, HALF, stride=dil)

            def window(ref):
                return jnp.concatenate([ref[0, 0, rows0, :], ref[0, 0, q_rows, :], ref[0, 0, rows3, :]], axis=0)

            q = q_ref[0, 0, q_rows, :].astype(_MXU_DTYPE)
            kw = window(k_ref).astype(_MXU_DTYPE)
            vw = window(v_ref).astype(_MXU_DTYPE)
            bias_b = bias_scr[...]
            pos = qb * bq - HALF + col
            ok = (bkt_ref[0] >= 0) & (pos >= 0) & (pos < sub_len)
            s = jnp.where(ok, _dot_nt(q, kw) * scale + bias_b, NEG)
            m_old = m_scr[q_rows, :]
            l_old = l_scr[q_rows, :]
            acc_old = o_ref[0, q_rows, :]
            m_new = jnp.maximum(m_old, jnp.max(s, axis=-1, keepdims=True))
            alpha = jnp.exp(m_old - m_new)
            p = jnp.exp(s - m_new[:, 0:1])
            m_scr[q_rows, :] = m_new
            l_scr[q_rows, :] = alpha * l_old + jnp.sum(p, axis=-1, keepdims=True)
            o_ref[0, q_rows, :] = alpha * acc_old + _dot(p.astype(_MXU_DTYPE), vw)
            return carry

        lax.fori_loop(0, dil * nb, body, 0)

    for gidx, (_, dil) in enumerate(B_GROUPS):
        pl.when(gi == gidx)(functools.partial(run, dil))

    @pl.when(gi == N_GROUPS - 1)
    def _():
        o_ref[0] = o_ref[0] / l_scr[...]


def _attention(rel_bias, buckets, q, k, v):
    bn, _, t, _ = q.shape
    qkv_spec = pl.BlockSpec((1, 1, t, B_HEAD_DIM), lambda b, h, g: (b, g * B_HPG + h, 0, 0))
    return pl.pallas_call(
        functools.partial(_attn_kernel, t=t),
        grid=(bn, B_HPG, N_GROUPS),
        in_specs=[
            pl.BlockSpec(memory_space=pltpu.SMEM),
            pl.BlockSpec((1, Q_BLOCK, 2 * Q_BLOCK), lambda b, h, g: (g, 0, 0)),
            qkv_spec, qkv_spec, qkv_spec,
        ],
        out_specs=pl.BlockSpec((1, t, B_HEAD_DIM), lambda b, h, g: (b, 0, h)),
        out_shape=jax.ShapeDtypeStruct((bn, t, B_WIDTH), F32),
        scratch_shapes=[
            pltpu.VMEM((t, B_HEAD_DIM), F32),
            pltpu.VMEM((t, B_HEAD_DIM), F32),
            pltpu.VMEM((Q_BLOCK, 2 * Q_BLOCK), F32),
        ],
        compiler_params=pltpu.CompilerParams(
            dimension_semantics=("parallel", "parallel", "arbitrary"), vmem_limit_bytes=VMEM_LIMIT_BYTES),
        name="attention",
    )(rel_bias, buckets, q, k, v)


def _merge_kernel(x_ref, mod_ref, y_ref, z_ref, o_ref, pg_ref, lw_ref, lb_ref, wa_ref, wb_ref, wo_ref, out_ref):
    aw = A_WIDTH
    y = y_ref[0, 0] + y_ref[0, 1]
    li = lax.broadcasted_iota(jnp.int32, (aw, aw), 0) // A_HEAD_DIM
    lj = lax.broadcasted_iota(jnp.int32, (aw, aw), 1) // A_HEAD_DIM
    head_mean = (li == lj).astype(F32) * (1.0 / A_HEAD_DIM)
    mean = _dot(y, head_mean, HIGHEST)
    yc = y - mean
    var = _dot(yc * yc, head_mean, HIGHEST)
    yn = yc * lax.rsqrt(var + GN_EPS) * lw_ref[...] + lb_ref[...]
    ya = yn * z_ref[0, 0] + z_ref[0, 1]
    pg = pg_ref[0]
    yb = o_ref[0] * _silu(pg[:, 0:B_WIDTH])
    gate_a = _sigmoid(pg[:, B_WIDTH:B_WIDTH + D_MODEL])
    gate_b = _sigmoid(pg[:, B_WIDTH + D_MODEL:])
    merged = (gate_a * _dot(ya.astype(_MXU_DTYPE), wa_ref[...])
              + gate_b * _dot(yb.astype(_MXU_DTYPE), wb_ref[...]))
    out_ref[0] = x_ref[0] + mod_ref[0, 2:3, :] * _dot(merged.astype(_MXU_DTYPE), wo_ref[...])


def _merge(x, mod3, y, z, o, pg, lnx_w, lnx_b, wa_b, wb_b, wo_b):
    bn, t, _ = x.shape
    tm = MERGE_ROWS
    const2 = lambda b, i: (0, 0)
    return pl.pallas_call(
        _merge_kernel,
        grid=(bn, t // tm),
        in_specs=[
            pl.BlockSpec((1, tm, D_MODEL), lambda b, i: (b, i, 0)),
            pl.BlockSpec((1, 3, D_MODEL), lambda b, i: (b, 0, 0)),
            pl.BlockSpec((1, 2, tm, A_WIDTH), lambda b, i: (b, 0, i, 0)),
            pl.BlockSpec((1, 2, tm, A_WIDTH), lambda b, i: (b, 0, i, 0)),
            pl.BlockSpec((1, tm, B_WIDTH), lambda b, i: (b, i, 0)),
            pl.BlockSpec((1, tm, PG_COLS), lambda b, i: (b, i, 0)),
            pl.BlockSpec((1, A_WIDTH), const2),
            pl.BlockSpec((1, A_WIDTH), const2),
            pl.BlockSpec((A_WIDTH, D_MODEL), const2),
            pl.BlockSpec((B_WIDTH, D_MODEL), const2),
            pl.BlockSpec((D_MODEL, D_MODEL), const2),
        ],
        out_specs=pl.BlockSpec((1, tm, D_MODEL), lambda b, i: (b, i, 0)),
        out_shape=jax.ShapeDtypeStruct((bn, t, D_MODEL), x.dtype),
        compiler_params=pltpu.CompilerParams(
            dimension_semantics=("parallel", "parallel"), vmem_limit_bytes=VMEM_LIMIT_BYTES),
        name="merge",
    )(x, mod3, y, z, o, pg, lnx_w.reshape(1, A_WIDTH), lnx_b.reshape(1, A_WIDTH), wa_b, wb_b, wo_b)


def _pad_low_rank(w):
    z = jnp.zeros((LOW_RANK, A_WIDTH), w.dtype)
    return jnp.stack([jnp.concatenate([w[0], z], axis=0), jnp.concatenate([z, w[1]], axis=0)])


def kernel(x_prompt, x_sample, c_prompt, c_sample, rel_bias, norm_w, w_ada, b_ada, w_in, shift_mu, w0, w2, a0, a2,
           k_k, k_a, r_k, lnx_w, lnx_b, q_gain, k_gain, w_out_a, w_out_b, w_out):
    depth = norm_w.shape[0]
    buckets = jnp.asarray(_band_buckets())

    def trunk(x, c):
        bn, t, _ = x.shape
        assert t % (Q_BLOCK * B_GROUPS[-1][1]) == 0 and t % MERGE_ROWS == 0
        for l in range(depth):
            mod3 = _modulation(c, w_ada[l], b_ada[l]).reshape(bn, 3, D_MODEL)
            pa, q, k, v, pg = _projection(x, mod3, norm_w[l], w_in[l].astype(_MXU_DTYPE), q_gain[l], k_gain[l])
            y, z = _rwkv(pa, shift_mu[l], w0[l], _pad_low_rank(w2[l]), a0[l], _pad_low_rank(a2[l]),
                         k_k[l], k_a[l], r_k[l].reshape(A_WIDTH))
            o = _attention(rel_bias, buckets, q, k, v)
            x = _merge(x, mod3, y, z, o, pg, lnx_w[l], lnx_b[l], w_out_a[l].astype(_MXU_DTYPE),
                       w_out_b[l].astype(_MXU_DTYPE), w_out[l].astype(_MXU_DTYPE))
        return x

    return (trunk(x_prompt, c_prompt), trunk(x_sample, c_sample))
```

```python
import functools
import math

import numpy as np
import jax
import jax.numpy as jnp
from jax import lax
from jax.experimental import pallas as pl
from jax.experimental.pallas import tpu as pltpu

F32 = jnp.float32
BF16 = jnp.bfloat16
HIGHEST = lax.Precision.HIGHEST

D_MODEL = 1024
A_HEADS = 8
A_HEAD_DIM = 64
A_WIDTH = A_HEADS * A_HEAD_DIM
LOW_RANK = 64
A_COLS = 4 * A_WIDTH + 4 * LOW_RANK
GN_EPS = 64e-5
B_GROUPS = ((128, 1), (512, 4), (2048, 16))
N_GROUPS = len(B_GROUPS)
B_HPG = 4
B_HEAD_DIM = 128
B_HEADS = N_GROUPS * B_HPG
B_QKV = B_HEADS * B_HEAD_DIM
B_WIDTH = B_HPG * B_HEAD_DIM
Q_BLOCK = 128
HALF = 64
N_BUCKETS = 32
MAX_EXACT = 8
MAX_DISTANCE = 1024
PG_COLS = B_WIDTH + 2 * D_MODEL
IN_COLS = A_COLS + 3 * B_QKV + PG_COLS
RMS_EPS = 1e-6
NEG = -1e30

LANES = 128
SUBLANES = 8
MXU_DIM = 256
VMEM_LIMIT_BYTES = 60000 * 1024

PROJ_ROWS = 256
CHUNK = 64
PREP_ROWS = 256
SCAN_SUB = 4
SCAN_GROUP = 4
HEADS_PER_STACK = MXU_DIM // CHUNK
STACK_LANES = HEADS_PER_STACK * A_HEAD_DIM
N_STACKS = A_HEADS // HEADS_PER_STACK
MERGE_ROWS = 512
ATTN_BLOCKS_PER_ITER = 4

_MXU_DTYPE = BF16


def _dot(a, b, precision=None):
    return jnp.dot(a, b, preferred_element_type=F32, precision=precision)


def _dot_nt(a, b, precision=None):
    return lax.dot_general(a, b, (((1,), (1,)), ((), ())), preferred_element_type=F32, precision=precision)


def _dot_tn(a, b, precision=None):
    return lax.dot_general(a, b, (((0,), (0,)), ((), ())), preferred_element_type=F32, precision=precision)


def _sigmoid(z):
    return 1.0 / (1.0 + jnp.exp(-z))


def _silu(z):
    return z * _sigmoid(z)


def _modulation_kernel(c_ref, w_ref, b_ref, o_ref):
    o_ref[...] = _dot(_silu(c_ref[...]), w_ref[...], HIGHEST) + b_ref[...]


def _modulation(c, w_ada, b_ada):
    bn = c.shape[0]
    return pl.pallas_call(
        _modulation_kernel,
        out_shape=jax.ShapeDtypeStruct((bn, 3 * D_MODEL), F32),
        compiler_params=pltpu.CompilerParams(vmem_limit_bytes=VMEM_LIMIT_BYTES),
        name="modulation",
    )(c, w_ada, b_ada.reshape(1, 3 * D_MODEL))


def _proj_kernel(x_ref, mod_ref, nw_ref, w_ref, qg_ref, kg_ref, pa_ref, q_ref, k_ref, v_ref, pg_ref):
    x = x_ref[0]
    ms = jnp.mean(x * x, axis=-1, keepdims=True)
    h = x * lax.rsqrt(ms + RMS_EPS) * nw_ref[...] * (1.0 + mod_ref[0, 1:2, :]) + mod_ref[0, 0:1, :]
    hb = h.astype(_MXU_DTYPE)
    step = 2 * MXU_DIM

    def cols(c0, width):
        return _dot(hb, w_ref[:, c0:c0 + width])

    for c0 in range(0, A_COLS, step):
        width = min(step, A_COLS - c0)
        pa_ref[0, :, c0:c0 + width] = cols(c0, width)

    def head_norm(z, gain):
        return z * lax.rsqrt(jnp.mean(z * z, axis=-1, keepdims=True) + RMS_EPS) * gain

    heads_per_step = step // B_HEAD_DIM
    for c0 in range(0, B_QKV, step):
        qc = cols(A_COLS + c0, step)
        kc = cols(A_COLS + B_QKV + c0, step)
        vc = cols(A_COLS + 2 * B_QKV + c0, step)
        for j in range(heads_per_step):
            hd = c0 // B_HEAD_DIM + j
            sl = slice(j * B_HEAD_DIM, (j + 1) * B_HEAD_DIM)
            q_ref[0, hd] = head_norm(qc[:, sl], qg_ref[...])
            k_ref[0, hd] = head_norm(kc[:, sl], kg_ref[...])
            v_ref[0, hd] = vc[:, sl]

    pg0 = A_COLS + 3 * B_QKV
    for c0 in range(0, PG_COLS, step):
        pg_ref[0, :, c0:c0 + step] = cols(pg0 + c0, step)


def _projection(x, mod3, norm_w, w_in_b, q_gain, k_gain):
    bn, t, _ = x.shape
    tm = PROJ_ROWS
    grid = (bn, t // tm)
    const2 = lambda b, i: (0, 0)
    return pl.pallas_call(
        _proj_kernel,
        grid=grid,
        in_specs=[
            pl.BlockSpec((1, tm, D_MODEL), lambda b, i: (b, i, 0)),
            pl.BlockSpec((1, 3, D_MODEL), lambda b, i: (b, 0, 0)),
            pl.BlockSpec((1, D_MODEL), const2),
            pl.BlockSpec((D_MODEL, IN_COLS), const2, pipeline_mode=pl.Buffered(1)),
            pl.BlockSpec((1, B_HEAD_DIM), const2),
            pl.BlockSpec((1, B_HEAD_DIM), const2),
        ],
        out_specs=[
            pl.BlockSpec((1, tm, A_COLS), lambda b, i: (b, i, 0)),
            pl.BlockSpec((1, B_HEADS, tm, B_HEAD_DIM), lambda b, i: (b, 0, i, 0)),
            pl.BlockSpec((1, B_HEADS, tm, B_HEAD_DIM), lambda b, i: (b, 0, i, 0)),
            pl.BlockSpec((1, B_HEADS, tm, B_HEAD_DIM), lambda b, i: (b, 0, i, 0)),
            pl.BlockSpec((1, tm, PG_COLS), lambda b, i: (b, i, 0)),
        ],
        out_shape=[
            jax.ShapeDtypeStruct((bn, t, A_COLS), F32),
            jax.ShapeDtypeStruct((bn, B_HEADS, t, B_HEAD_DIM), F32),
            jax.ShapeDtypeStruct((bn, B_HEADS, t, B_HEAD_DIM), F32),
            jax.ShapeDtypeStruct((bn, B_HEADS, t, B_HEAD_DIM), F32),
            jax.ShapeDtypeStruct((bn, t, PG_COLS), F32),
        ],
        compiler_params=pltpu.CompilerParams(
            dimension_semantics=("parallel", "parallel"), vmem_limit_bytes=VMEM_LIMIT_BYTES),
        name="projection",
    )(x, mod3, norm_w.reshape(1, D_MODEL), w_in_b, q_gain.reshape(1, B_HEAD_DIM), k_gain.reshape(1, B_HEAD_DIM))


def _split3(z):
    hi = z.astype(BF16)
    r1 = z - hi.astype(F32)
    mid = r1.astype(BF16)
    lo = (r1 - mid.astype(F32)).astype(BF16)
    return hi, mid, lo


def _dot_exact_lhs(lhs_b, z):
    hi, mid, lo = _split3(z)
    return _dot(lhs_b, hi) + (_dot(lhs_b, mid) + _dot(lhs_b, lo))


def _head_sums(z, head_ones_b):
    hi = z.astype(BF16)
    lo = (z - hi.astype(F32)).astype(BF16)
    return _dot(hi, head_ones_b) + _dot(lo, head_ones_b)


def _head_ones():
    head_of = np.arange(A_WIDTH) // A_HEAD_DIM
    return jnp.asarray(head_of[:, None] == head_of[None, :], dtype=BF16)


_MASK_STRICT, _MASK_INCL, _MASK_EYE, _MASK_LEVEL0 = 0, 1, 2, 3
_N_LEVELS = int(math.log2(CHUNK))
_N_MASKS = _MASK_LEVEL0 + _N_LEVELS
_OP_A, _OP_R, _OP_B, _OP_K, _OP_BTAIL, _OP_KTAIL = range(6)
_N_OPS = 6


def _scan_masks():
    idx = np.arange(MXU_DIM)
    out = []
    for d in range(2):
        p = idx % CHUNK if d == 0 else CHUNK - 1 - idx % CHUNK
        pi, pj = p[:, None], p[None, :]
        masks = [pj < pi, pj <= pi, idx[:, None] == idx[None, :]]
        for lv in range(_N_LEVELS):
            sz = 1 << lv
            masks.append(((pi // sz) % 2 == 1) & (pj // sz == pi // sz - 1))
        out.append(np.stack(masks))
    return np.stack(out).astype(np.float32)


def _cumulative_lhs():
    idx = np.arange(PREP_ROWS)
    same = idx[:, None] // CHUNK == idx[None, :] // CHUNK
    fwd = same & (idx[None, :] <= idx[:, None])
    bwd = same & (idx[None, :] >= idx[:, None])
    return np.stack([np.concatenate([fwd, same]), np.concatenate([bwd, same])]).astype(np.float32)


def _prep_kernel(pa_ref, prev_ref, next_ref, mu_ref, w0_ref, w2_ref, a0_ref, a2_ref, kk_ref, ka_ref, rk_ref, ones_ref,
                 cum_ref, ops_ref, v_ref, pend_ref, z_ref):
    i = pl.program_id(1)
    n = pl.num_programs(1)
    c = CHUNK
    rb = PREP_ROWS

    pa = pa_ref[0]
    prev_row = jnp.where(i > 0, prev_ref[0, SUBLANES - 1:SUBLANES, :], 0.0)
    next_row = jnp.where(i < n - 1, next_ref[0, 0:1, :], 0.0)
    row = lax.broadcasted_iota(jnp.int32, (rb, 1), 0)
    prev = jnp.where(row == 0, prev_row, pltpu.roll(pa, 1, 0))
    nxt = jnp.where(row == rb - 1, next_row, pltpu.roll(pa, rb - 1, 0))
    xs = pa + mu_ref[0:1, :] * (prev - pa) + mu_ref[1:2, :] * (nxt - pa)

    aw = A_WIDTH
    r = xs[:, 0:aw]
    k = xs[:, aw:2 * aw]
    v = xs[:, 2 * aw:3 * aw]
    g = xs[:, 3 * aw:4 * aw]
    tw = jnp.tanh(xs[:, 4 * aw:4 * aw + 2 * LOW_RANK]).astype(BF16)
    alo = xs[:, 4 * aw + 2 * LOW_RANK:4 * aw + 4 * LOW_RANK].astype(BF16)

    head_ones = ones_ref[...]
    kk = k * kk_ref[...]
    kk = kk / jnp.maximum(jnp.sqrt(_head_sums(kk * kk, head_ones)), 1e-12)
    ka = ka_ref[...]
    a_dir = [_sigmoid(a0_ref[d] + _dot(alo, a2_ref[d])) for d in range(2)]
    k_both = k * (2.0 + (a_dir[0] + a_dir[1] - 2.0) * ka)
    bonus = _head_sums(r * k_both * rk_ref[...], head_ones) * v
    sg = _silu(g)
    z_ref[0, 0] = sg
    z_ref[0, 1] = bonus * sg
    v_ref[0] = v.astype(BF16)

    for d in range(2):
        u = w0_ref[d] + _dot(tw, w2_ref[d])
        logw = -math.exp(-0.5) * _sigmoid(u)
        sums = _dot_exact_lhs(cum_ref[d], logw)
        cl = sums[0:rb]
        total = sums[rb:2 * rb]
        e_out = jnp.exp(-cl)
        e_tail = jnp.exp(total - cl)
        k_d = k * (1.0 + (a_dir[d] - 1.0) * ka)
        b_d = kk * a_dir[d]
        ops_ref[0, d, _OP_A] = (-kk * jnp.exp(cl - logw)).astype(BF16)
        ops_ref[0, d, _OP_R] = (r * jnp.exp(cl)).astype(BF16)
        ops_ref[0, d, _OP_B] = (b_d * e_out).astype(BF16)
        ops_ref[0, d, _OP_K] = (k_d * e_out).astype(BF16)
        ops_ref[0, d, _OP_BTAIL] = (b_d * e_tail).astype(BF16)
        ops_ref[0, d, _OP_KTAIL] = (k_d * e_tail).astype(BF16)
        p_end = jnp.exp(total)
        for j in range(rb // c):
            pend_ref[0, d, j] = p_end[j * c:j * c + 1, :]


def _prep(pa, shift_mu, w0, w2p, a0, a2p, k_k, k_a, r_k):
    bn, t, _ = pa.shape
    rb = PREP_ROWS
    n = t // rb
    rows8 = rb // SUBLANES
    head_ones = _head_ones()
    cum_lhs =jnp.asarray(_cumulative_lhs(), dtype=BF16)
    const2 = lambda b, i: (0, 0)
    const3 = lambda b, i: (0, 0, 0)
    return pl.pallas_call(
        _prep_kernel,
        grid=(bn, n),
        in_specs=[
            pl.BlockSpec((1, rb, A_COLS), lambda b, i: (b, i, 0)),
            pl.BlockSpec((1, SUBLANES, A_COLS), lambda b, i: (b, jnp.maximum(i * rows8 - 1, 0), 0)),
            pl.BlockSpec((1, SUBLANES, A_COLS), lambda b, i: (b, jnp.minimum((i + 1) * rows8, t // SUBLANES - 1), 0)),
            pl.BlockSpec((2, A_COLS), const2),
            pl.BlockSpec((2, 1, A_WIDTH), const3),
            pl.BlockSpec((2, 2 * LOW_RANK, A_WIDTH), const3),
            pl.BlockSpec((2, 1, A_WIDTH), const3),
            pl.BlockSpec((2, 2 * LOW_RANK, A_WIDTH), const3),
            pl.BlockSpec((1, A_WIDTH), const2),
            pl.BlockSpec((1, A_WIDTH), const2),
            pl.BlockSpec((1, A_WIDTH), const2),
            pl.BlockSpec((A_WIDTH, A_WIDTH), const2),
            pl.BlockSpec((2, 2 * rb, rb), const3),
        ],
        out_specs=[
            pl.BlockSpec((1, 2, _N_OPS, rb, A_WIDTH), lambda b, i: (b, 0, 0, i, 0)),
            pl.BlockSpec((1, rb, A_WIDTH), lambda b, i: (b, i, 0)),
            pl.BlockSpec((1, 2, rb // CHUNK, 1, A_WIDTH), lambda b, i: (b, 0, i, 0, 0)),
            pl.BlockSpec((1, 2, rb, A_WIDTH), lambda b, i: (b, 0, i, 0)),
        ],
        out_shape=[
            jax.ShapeDtypeStruct((bn, 2, _N_OPS, t, A_WIDTH), BF16),
            jax.ShapeDtypeStruct((bn, t, A_WIDTH), BF16),
            jax.ShapeDtypeStruct((bn, 2, t // CHUNK, 1, A_WIDTH), F32),
            jax.ShapeDtypeStruct((bn, 2, t, A_WIDTH), F32),
        ],
        compiler_params=pltpu.CompilerParams(
            dimension_semantics=("parallel", "parallel"), vmem_limit_bytes=VMEM_LIMIT_BYTES),
        name="rwkv_prep",
    )(pa, pa, pa, shift_mu, w0.reshape(2, 1, A_WIDTH), w2p.astype(BF16), a0.reshape(2, 1, A_WIDTH),
      a2p.astype(BF16), k_k.reshape(1, A_WIDTH), k_a.reshape(1, A_WIDTH), r_k.reshape(1, A_WIDTH), head_ones, cum_lhs)


def _scan_kernel(ops_ref, v_ref, pend_ref, mask_ref, y_ref, h_scr):
    d = pl.program_id(1)
    i = pl.program_id(2)
    fwd = d == 0
    c = CHUNK

    @pl.when(i == 0)
    def _():
        h_scr[...] = jnp.zeros_like(h_scr)

    lane_head = lax.broadcasted_iota(jnp.int32, (1, STACK_LANES), 1) // A_HEAD_DIM
    head_sel = [lane_head == hh for hh in range(HEADS_PER_STACK)]
    strict = mask_ref[0, _MASK_STRICT] != 0.0
    incl = mask_ref[0, _MASK_INCL] != 0.0
    eye = mask_ref[0, _MASK_EYE]

    def stack(ref_slice):
        return jnp.concatenate([jnp.where(head_sel[hh], ref_slice, jnp.zeros_like(ref_slice))
                                for hh in range(HEADS_PER_STACK)], axis=0)

    for j0 in range(0, SCAN_SUB, SCAN_GROUP):
        probs = []
        for j in range(j0, j0 + SCAN_GROUP):
            jj = jnp.where(fwd, j, SCAN_SUB - 1 - j)
            rows = pl.ds(pl.multiple_of(jj * c, c), c)
            for s in range(N_STACKS):
                probs.append((jj, rows, s, slice(s * STACK_LANES, (s + 1) * STACK_LANES)))
        todo = range(len(probs))

        st = []
        for jj, rows, s, sl in probs:
            st.append({name: stack(ops_ref[0, 0, op, rows, sl]) for name, op in
                       (("a", _OP_A), ("r", _OP_R), ("b", _OP_B), ("k", _OP_K), ("bt", _OP_BTAIL), ("kt", _OP_KTAIL))})
            st[-1]["v"] = stack(v_ref[0, rows, sl])
        l_ab = [jnp.where(strict, _dot_nt(q["a"], q["b"]), 0.0) for q in st]
        l_ak = [jnp.where(strict, _dot_nt(q["a"], q["k"]), 0.0).astype(BF16) for q in st]
        m_rb = [jnp.where(incl, _dot_nt(q["r"], q["b"]), 0.0).astype(BF16) for q in st]
        m_rk = [jnp.where(incl, _dot_nt(q["r"], q["k"]), 0.0).astype(BF16) for q in st]
        inv = [eye + l * mask_ref[0, _MASK_LEVEL0] for l in l_ab]
        for lv in range(1, _N_LEVELS):
            inv_b = [x.astype(BF16) for x in inv]
            lo = [(l * mask_ref[0, _MASK_LEVEL0 + lv]).astype(BF16) for l in l_ab]
            tmp = [_dot(lo[p], inv_b[p]).astype(BF16) for p in todo]
            inv = [inv[p] + _dot(inv_b[p], tmp[p]) for p in todo]
        inv_b = [x.astype(BF16) for x in inv]
        a_hat = [_dot(inv_b[p], st[p]["a"]).astype(BF16) for p in todo]
        w = [_dot(l_ak[p], st[p]["v"]).astype(BF16) for p in todo]
        u0 = [_dot(inv_b[p], w[p]).astype(BF16) for p in todo]
        r_hat = [(st[p]["r"].astype(F32) + _dot(m_rb[p], a_hat[p])).astype(BF16) for p in todo]
        y0 = [_dot(m_rb[p], u0[p]) + _dot(m_rk[p], st[p]["v"]) for p in todo]
        gmat = [(eye * pend_ref[0, 0, probs[p][0]][:, probs[p][3]] + _dot_tn(st[p]["bt"], a_hat[p])).astype(BF16)
                for p in todo]
        h_add = [_dot_tn(st[p]["bt"], u0[p]) + _dot_tn(st[p]["kt"], st[p]["v"]) for p in todo]

        for p, (jj, rows, s, sl) in enumerate(probs):
            h0 = h_scr[s].astype(BF16)
            y_s = _dot(r_hat[p], h0) + y0[p]
            h_scr[s] = _dot(gmat[p], h0) + h_add[p]
            y = y_s[0:c]
            for hh in range(1, HEADS_PER_STACK):
                y = y + y_s[hh * c:(hh + 1) * c]
            y_ref[0, 0, rows, sl] = y


def _scan(ops, v, pend):
    bn, _, _, t, _ = ops.shape
    rb = SCAN_SUB * CHUNK
    n = t // rb
    masks = jnp.asarray(_scan_masks())

    def block_of(d, i):
        return jnp.where(d == 0, i, n - 1 - i)

    return pl.pallas_call(
        _scan_kernel,
        grid=(bn, 2, n),
        in_specs=[
            pl.BlockSpec((1, 1, _N_OPS, rb, A_WIDTH), lambda b, d, i: (b, d, 0, block_of(d, i), 0)),
            pl.BlockSpec((1, rb, A_WIDTH), lambda b, d, i: (b, block_of(d, i), 0)),
            pl.BlockSpec((1, 1, SCAN_SUB, 1, A_WIDTH), lambda b, d, i: (b, d, block_of(d, i), 0, 0)),
            pl.BlockSpec((1, _N_MASKS, MXU_DIM, MXU_DIM), lambda b, d, i: (d, 0, 0, 0)),
        ],
        out_specs=pl.BlockSpec((1, 1, rb, A_WIDTH), lambda b, d, i: (b, d, block_of(d, i), 0)),
        out_shape=jax.ShapeDtypeStruct((bn, 2, t, A_WIDTH), F32),
        scratch_shapes=[pltpu.VMEM((N_STACKS, MXU_DIM, MXU_DIM), F32)],
        compiler_params=pltpu.CompilerParams(
            dimension_semantics=("parallel", "arbitrary", "arbitrary"), vmem_limit_bytes=VMEM_LIMIT_BYTES),
        name="rwkv_scan",
    )(ops, v, pend, masks)


def _rwkv(pa, shift_mu, w0, w2p, a0, a2p, k_k, k_a, r_k):
    ops, v, pend, z = _prep(pa, shift_mu, w0, w2p, a0, a2p, k_k, k_a, r_k)
    return _scan(ops, v, pend), z


def _t5_bucket_np(rel):
    half_b = N_BUCKETS // 2
    ret = np.where(rel > 0, half_b, 0)
    n = np.abs(rel)
    large = MAX_EXACT + (np.log(np.maximum(n, 1).astype(np.float32) / MAX_EXACT)
                         / math.log(MAX_DISTANCE / MAX_EXACT) * (half_b - MAX_EXACT)).astype(np.int32)
    large = np.minimum(large, half_b - 1)
    return ret + np.where(n < MAX_EXACT, n, large)


def _band_buckets():
    i = np.arange(Q_BLOCK)[:, None]
    j = np.arange(2 * Q_BLOCK)[None, :]
    rel = j - HALF - i
    out = []
    for _, dil in B_GROUPS:
        bkt = _t5_bucket_np(rel * dil)
        out.append(np.where(np.abs(rel) <= HALF, bkt, -1))
    return np.stack(out).astype(np.int32)


def _attn_kernel(tbl_ref, bkt_ref, q_ref, k_ref, v_ref, o_ref, m_scr, l_scr, bias_scr, *, t):
    h4 = pl.program_id(1)
    gi = pl.program_id(2)
    head = gi * B_HPG + h4
    qb = Q_BLOCK
    scale = B_HEAD_DIM ** -0.5

    @pl.when(gi == 0)
    def _():
        m_scr[...] = jnp.full_like(m_scr, NEG)
        l_scr[...] = jnp.zeros_like(l_scr)
        o_ref[...] = jnp.zeros_like(o_ref)

    bkt = bkt_ref[0]
    bias = jnp.full((qb, 2 * qb), NEG, F32)
    for bb in range(N_BUCKETS):
        bias = jnp.where(bkt == bb, tbl_ref[bb, head], bias)
    bias_scr[...] = bias
    col = lax.broadcasted_iota(jnp.int32, (1, 2 * qb), 1)

    def run(dil):
        sub_len = t // dil
        nb = sub_len // qb

        def body(it, carry):
            todo = range(ATTN_BLOCKS_PER_ITER)
            q_rows, qs, kws, vws, valid = [], [], [], [], []
            for u in todo:
                blk = it * ATTN_BLOCKS_PER_ITER + u
                rho = blk // nb
                bq = blk % nb
                rows = pl.ds(rho + dil * qb * bq, qb, stride=dil)
                s0 = jnp.maximum(qb * bq - HALF, 0)
                s3 = jnp.minimum(qb * bq + qb, sub_len - HALF)
                rows0 = pl.ds(rho + dil * s0, HALF, stride=dil)
                rows3 = pl.ds(rho + dil * s3, HALF, stride=dil)

                def window(ref):
                    return jnp.concatenate([ref[0, 0, rows0, :], ref[0, 0, rows, :], ref[0, 0, rows3, :]], axis=0)

                q_rows.append(rows)
                qs.append(q_ref[0, 0, rows, :].astype(_MXU_DTYPE))
                kws.append(window(k_ref).astype(_MXU_DTYPE))
                vws.append(window(v_ref).astype(_MXU_DTYPE))
                pos = qb * bq - HALF + col
                valid.append((pos >= 0) & (pos < sub_len))
            bias_b = bias_scr[...]
            s = [jnp.where(valid[u], _dot_nt(qs[u], kws[u]) * scale + bias_b, NEG) for u in todo]
            m_blk = [jnp.max(s[u], axis=-1, keepdims=True) for u in todo]
            m_old = [m_scr[q_rows[u], :] for u in todo]
            l_old = [l_scr[q_rows[u], :] for u in todo]
            acc_old = [o_ref[0, q_rows[u], :] for u in todo]
            m_new = [jnp.maximum(m_old[u], m_blk[u]) for u in todo]
            alpha = [jnp.exp(m_old[u] - m_new[u]) for u in todo]
            p = [jnp.exp(s[u] - m_new[u][:, 0:1]) for u in todo]
            l_new = [alpha[u] * l_old[u] + jnp.sum(p[u], axis=-1, keepdims=True) for u in todo]
            acc = [alpha[u] * acc_old[u] + _dot(p[u].astype(_MXU_DTYPE), vws[u]) for u in todo]
            for u in todo:
                m_scr[q_rows[u], :] = m_new[u]
                l_scr[q_rows[u], :] = l_new[u]
                o_ref[0, q_rows[u], :] = acc[u]
            return carry

        lax.fori_loop(0, dil * nb // ATTN_BLOCKS_PER_ITER, body, 0)

    for gidx, (_, dil) in enumerate(B_GROUPS):
        pl.when(gi == gidx)(functools.partial(run, dil))

    @pl.when(gi == N_GROUPS - 1)
    def _():
        o_ref[0] = o_ref[0] / l_scr[...]


def _attention(rel_bias, buckets, q, k, v):
    bn, _, t, _ = q.shape
    qkv_spec = pl.BlockSpec((1, 1, t, B_HEAD_DIM), lambda b, h, g: (b, g * B_HPG + h, 0, 0))
    return pl.pallas_call(
        functools.partial(_attn_kernel, t=t),
        grid=(bn, B_HPG, N_GROUPS),
        in_specs=[
            pl.BlockSpec(memory_space=pltpu.SMEM),
            pl.BlockSpec((1, Q_BLOCK, 2 * Q_BLOCK), lambda b, h, g: (g, 0, 0)),
            qkv_spec, qkv_spec, qkv_spec,
        ],
        out_specs=pl.BlockSpec((1, t, B_HEAD_DIM), lambda b, h, g: (b, 0, h)),
        out_shape=jax.ShapeDtypeStruct((bn, t, B_WIDTH), F32),
        scratch_shapes=[
            pltpu.VMEM((t, B_HEAD_DIM), F32),
            pltpu.VMEM((t, B_HEAD_DIM), F32),
            pltpu.VMEM((Q_BLOCK, 2 * Q_BLOCK), F32),
        ],
        compiler_params=pltpu.CompilerParams(
            dimension_semantics=("parallel", "parallel", "arbitrary"), vmem_limit_bytes=VMEM_LIMIT_BYTES),
        name="attention",
    )(rel_bias, buckets, q, k, v)


def _merge_kernel(x_ref, mod_ref, y_ref, z_ref, o_ref, pg_ref, lw_ref, lb_ref, ones_ref, wa_ref, wb_ref, wo_ref,
                  out_ref):
    y =y_ref[0, 0] + y_ref[0, 1]
    head_ones = ones_ref[...]
    mean = _head_sums(y, head_ones) * (1.0 / A_HEAD_DIM)
    yc = y - mean
    var = _head_sums(yc * yc, head_ones) * (1.0 / A_HEAD_DIM)
    yn = yc * lax.rsqrt(var + GN_EPS) * lw_ref[...] + lb_ref[...]
    ya = yn * z_ref[0, 0] + z_ref[0, 1]
    pg = pg_ref[0]
    yb = o_ref[0] * _silu(pg[:, 0:B_WIDTH])
    gate_a = _sigmoid(pg[:, B_WIDTH:B_WIDTH + D_MODEL])
    gate_b = _sigmoid(pg[:, B_WIDTH + D_MODEL:])
    merged = (gate_a * _dot(ya.astype(_MXU_DTYPE), wa_ref[...])
              + gate_b * _dot(yb.astype(_MXU_DTYPE), wb_ref[...]))
    out_ref[0] = x_ref[0] + mod_ref[0, 2:3, :] * _dot(merged.astype(_MXU_DTYPE), wo_ref[...])


def _merge(x, mod3, y, z, o, pg, lnx_w, lnx_b, wa_b, wb_b, wo_b):
    bn, t, _ = x.shape
    tm = MERGE_ROWS
    const2 = lambda b, i: (0, 0)
    return pl.pallas_call(
        _merge_kernel,
        grid=(bn, t // tm),
        in_specs=[
            pl.BlockSpec((1, tm, D_MODEL), lambda b, i: (b, i, 0)),
            pl.BlockSpec((1, 3, D_MODEL), lambda b, i: (b, 0, 0)),
            pl.BlockSpec((1, 2, tm, A_WIDTH), lambda b, i: (b, 0, i, 0)),
            pl.BlockSpec((1, 2, tm, A_WIDTH), lambda b, i: (b, 0, i, 0)),
            pl.BlockSpec((1, tm, B_WIDTH), lambda b, i: (b, i, 0)),
            pl.BlockSpec((1, tm, PG_COLS), lambda b, i: (b, i, 0)),
            pl.BlockSpec((1, A_WIDTH), const2),
            pl.BlockSpec((1, A_WIDTH), const2),
            pl.BlockSpec((A_WIDTH, A_WIDTH), const2),
            pl.BlockSpec((A_WIDTH, D_MODEL), const2),
            pl.BlockSpec((B_WIDTH, D_MODEL), const2),
            pl.BlockSpec((D_MODEL, D_MODEL), const2),
        ],
        out_specs=pl.BlockSpec((1, tm, D_MODEL), lambda b, i: (b, i, 0)),
        out_shape=jax.ShapeDtypeStruct((bn, t, D_MODEL), x.dtype),
        compiler_params=pltpu.CompilerParams(
            dimension_semantics=("parallel", "parallel"), vmem_limit_bytes=VMEM_LIMIT_BYTES),
        name="merge",
    )(x, mod3, y, z, o, pg, lnx_w.reshape(1, A_WIDTH), lnx_b.reshape(1, A_WIDTH), _head_ones(), wa_b, wb_b,
      wo_b)


def _pad_low_rank(w):
    z = jnp.zeros((LOW_RANK, A_WIDTH), w.dtype)
    return jnp.stack([jnp.concatenate([w[0], z], axis=0), jnp.concatenate([z, w[1]], axis=0)])


def kernel(x_prompt, x_sample, c_prompt, c_sample, rel_bias, norm_w, w_ada, b_ada, w_in, shift_mu, w0, w2, a0, a2,
           k_k, k_a, r_k, lnx_w, lnx_b, q_gain, k_gain, w_out_a, w_out_b, w_out):
    depth = norm_w.shape[0]
    buckets = jnp.asarray(_band_buckets())

    def trunk(x, c):
        bn, t, _ = x.shape
        assert t % (Q_BLOCK * B_GROUPS[-1][1]) == 0 and t % MERGE_ROWS == 0
        for l in range(depth):
            mod3 = _modulation(c, w_ada[l], b_ada[l]).reshape(bn, 3, D_MODEL)
            pa, q, k, v, pg = _projection(x, mod3, norm_w[l], w_in[l].astype(_MXU_DTYPE), q_gain[l], k_gain[l])
            y, z = _rwkv(pa, shift_mu[l], w0[l], _pad_low_rank(w2[l]), a0[l], _pad_low_rank(a2[l]),
                         k_k[l], k_a[l], r_k[l].reshape(A_WIDTH))
            o = _attention(rel_bias, buckets, q, k, v)
            x = _merge(x, mod3, y, z, o, pg, lnx_w[l], lnx_b[l], w_out_a[l].astype(_MXU_DTYPE),
                       w_out_b[l].astype(_MXU_DTYPE), w_out[l].astype(_MXU_DTYPE))
        return x

    return (trunk(x_prompt, c_prompt), trunk(x_sample, c_sample))
```

```python
import functools
import math

import numpy as np
import jax
import jax.numpy as jnp
from jax import lax
from jax.experimental import pallas as pl
from jax.experimental.pallas import tpu as pltpu

F32 = jnp.float32
BF16 = jnp.bfloat16
HIGHEST = lax.Precision.HIGHEST

D_MODEL = 1024
A_HEADS = 8
A_HEAD_DIM = 64
A_WIDTH = A_HEADS * A_HEAD_DIM
LOW_RANK = 64
A_COLS = 4 * A_WIDTH + 4 * LOW_RANK
GN_EPS = 64e-5
B_GROUPS = ((128, 1), (512, 4), (2048, 16))
N_GROUPS = len(B_GROUPS)
B_HPG = 4
B_HEAD_DIM = 128
B_HEADS = N_GROUPS * B_HPG
B_QKV = B_HEADS * B_HEAD_DIM
B_WIDTH = B_HPG * B_HEAD_DIM
Q_BLOCK = 128
HALF = 64
N_BUCKETS = 32
MAX_EXACT = 8
MAX_DISTANCE = 1024
PG_COLS = B_WIDTH + 2 * D_MODEL
IN_COLS = A_COLS + 3 * B_QKV + PG_COLS
RMS_EPS = 1e-6
NEG = -1e30

LANES = 128
SUBLANES = 8
MXU_DIM = 256
VMEM_LIMIT_BYTES = 60000 * 1024

PROJ_ROWS = 256
CHUNK = 64
PREP_ROWS = 256
SCAN_SUB = 4
HEADS_PER_STACK = MXU_DIM // CHUNK
STACK_LANES = HEADS_PER_STACK * A_HEAD_DIM
N_STACKS = A_HEADS // HEADS_PER_STACK
MERGE_ROWS = 512
ATTN_BLOCKS_PER_ITER = 4

_MXU_DTYPE = BF16
_ACT_DTYPE = BF16


def _dot(a, b, precision=None):
    return jnp.dot(a, b, preferred_element_type=F32, precision=precision)


def _dot_nt(a, b, precision=None):
    return lax.dot_general(a, b, (((1,), (1,)), ((), ())), preferred_element_type=F32, precision=precision)


def _dot_tn(a, b, precision=None):
    return lax.dot_general(a, b, (((0,), (0,)), ((), ())), preferred_element_type=F32, precision=precision)


def _sigmoid(z):
    return 1.0 / (1.0 + jnp.exp(-z))


def _silu(z):
    return z * _sigmoid(z)


def _modulation_kernel(c_ref, w_ref, b_ref, o_ref):
    o_ref[...] = _dot(_silu(c_ref[...]), w_ref[...], HIGHEST) + b_ref[...]


def _modulation(c, w_ada, b_ada):
    bn = c.shape[0]
    return pl.pallas_call(
        _modulation_kernel,
        out_shape=jax.ShapeDtypeStruct((bn, 3 * D_MODEL), F32),
        compiler_params=pltpu.CompilerParams(vmem_limit_bytes=VMEM_LIMIT_BYTES),
        name="modulation",
    )(c, w_ada, b_ada.reshape(1, 3 * D_MODEL))


def _proj_kernel(x_ref, mod_ref, nw_ref, w_ref, qg_ref, kg_ref, pa_ref, q_ref, k_ref, v_ref, pg_ref):
    x = x_ref[0]
    ms = jnp.mean(x * x, axis=-1, keepdims=True)
    h = x * lax.rsqrt(ms + RMS_EPS) * nw_ref[...] * (1.0 + mod_ref[0, 1:2, :]) + mod_ref[0, 0:1, :]
    hb = h.astype(_MXU_DTYPE)
    step = 2 * MXU_DIM

    def cols(c0, width):
        return _dot(hb, w_ref[:, c0:c0 + width])

    for c0 in range(0, A_COLS, step):
        width = min(step, A_COLS - c0)
        pa_ref[0, :, c0:c0 + width] = cols(c0, width)

    def head_norm(z, gain):
        return z * lax.rsqrt(jnp.mean(z * z, axis=-1, keepdims=True) + RMS_EPS) * gain

    heads_per_step = step // B_HEAD_DIM
    for c0 in range(0, B_QKV, step):
        qc = cols(A_COLS + c0, step)
        kc = cols(A_COLS + B_QKV + c0, step)
        vc = cols(A_COLS + 2 * B_QKV + c0, step)
        for j in range(heads_per_step):
            hd = c0 // B_HEAD_DIM + j
            sl = slice(j * B_HEAD_DIM, (j + 1) * B_HEAD_DIM)
            q_ref[0, hd] = head_norm(qc[:, sl], qg_ref[...])
            k_ref[0, hd] = head_norm(kc[:, sl], kg_ref[...])
            v_ref[0, hd] = vc[:, sl]

    pg0 = A_COLS + 3 * B_QKV
    for c0 in range(0, PG_COLS, step):
        pg_ref[0, :, c0:c0 + step] = cols(pg0 + c0, step).astype(_ACT_DTYPE)


def _projection(x, mod3, norm_w, w_in_b, q_gain, k_gain):
    bn, t, _ = x.shape
    tm = PROJ_ROWS
    grid = (bn, t // tm)
    const2 = lambda b, i: (0, 0)
    return pl.pallas_call(
        _proj_kernel,
        grid=grid,
        in_specs=[
            pl.BlockSpec((1, tm, D_MODEL), lambda b, i: (b, i, 0)),
            pl.BlockSpec((1, 3, D_MODEL), lambda b, i: (b, 0, 0)),
            pl.BlockSpec((1, D_MODEL), const2),
            pl.BlockSpec((D_MODEL, IN_COLS), const2, pipeline_mode=pl.Buffered(1)),
            pl.BlockSpec((1, B_HEAD_DIM), const2),
            pl.BlockSpec((1, B_HEAD_DIM), const2),
        ],
        out_specs=[
            pl.BlockSpec((1, tm, A_COLS), lambda b, i: (b, i, 0)),
            pl.BlockSpec((1, B_HEADS, tm, B_HEAD_DIM), lambda b, i: (b, 0, i, 0)),
            pl.BlockSpec((1, B_HEADS, tm, B_HEAD_DIM), lambda b, i: (b, 0, i, 0)),
            pl.BlockSpec((1, B_HEADS, tm, B_HEAD_DIM), lambda b, i: (b, 0, i, 0)),
            pl.BlockSpec((1, tm, PG_COLS), lambda b, i: (b, i, 0)),
        ],
        out_shape=[
            jax.ShapeDtypeStruct((bn, t, A_COLS), F32),
            jax.ShapeDtypeStruct((bn, B_HEADS, t, B_HEAD_DIM), F32),
            jax.ShapeDtypeStruct((bn, B_HEADS, t, B_HEAD_DIM), F32),
            jax.ShapeDtypeStruct((bn, B_HEADS, t, B_HEAD_DIM), F32),
            jax.ShapeDtypeStruct((bn, t, PG_COLS), _ACT_DTYPE),
        ],
        compiler_params=pltpu.CompilerParams(
            dimension_semantics=("parallel", "parallel"), vmem_limit_bytes=VMEM_LIMIT_BYTES),
        name="projection",
    )(x, mod3, norm_w.reshape(1, D_MODEL), w_in_b, q_gain.reshape(1, B_HEAD_DIM), k_gain.reshape(1, B_HEAD_DIM))


def _split3(z):
    hi = z.astype(BF16)
    r1 = z - hi.astype(F32)
    mid = r1.astype(BF16)
    lo = (r1 - mid.astype(F32)).astype(BF16)
    return hi, mid, lo


def _dot_exact_lhs(lhs_b, z):
    hi, mid, lo = _split3(z)
    return _dot(lhs_b, hi) + (_dot(lhs_b, mid) + _dot(lhs_b, lo))


def _head_sums(z, head_ones_b):
    hi = z.astype(BF16)
    lo = (z - hi.astype(F32)).astype(BF16)
    return _dot(hi, head_ones_b) + _dot(lo, head_ones_b)


def _head_ones():
    head_of = np.arange(A_WIDTH) // A_HEAD_DIM
    return jnp.asarray(head_of[:, None] == head_of[None, :], dtype=BF16)


_MASK_STRICT, _MASK_INCL, _MASK_EYE, _MASK_LEVEL0 = 0, 1, 2, 3
_N_LEVELS = int(math.log2(CHUNK))
_N_MASKS = _MASK_LEVEL0 + _N_LEVELS
_OP_A, _OP_R, _OP_B, _OP_K, _OP_BTAIL, _OP_KTAIL = range(6)
_N_OPS = 6


def _scan_masks():
    idx = np.arange(MXU_DIM)
    out = []
    for d in range(2):
        p = idx % CHUNK if d == 0 else CHUNK - 1 - idx % CHUNK
        pi, pj = p[:, None], p[None, :]
        masks = [pj < pi, pj <= pi, idx[:, None] == idx[None, :]]
        for lv in range(_N_LEVELS):
            sz = 1 << lv
            masks.append(((pi // sz) % 2 == 1) & (pj // sz == pi // sz - 1))
        out.append(np.stack(masks))
    return np.stack(out).astype(np.float32)


def _cumulative_lhs():
    idx = np.arange(PREP_ROWS)
    same = idx[:, None] // CHUNK == idx[None, :] // CHUNK
    fwd = same & (idx[None, :] <= idx[:, None])
    bwd = same & (idx[None, :] >= idx[:, None])
    return np.stack([np.concatenate([fwd, same]), np.concatenate([bwd, same])]).astype(np.float32)


def _prep_kernel(pa_ref, prev_ref, next_ref, mu_ref, w0_ref, w2_ref, a0_ref, a2_ref, kk_ref, ka_ref, rk_ref, ones_ref,
                 cum_ref, ops_ref, v_ref, pend_ref, z_ref):
    i = pl.program_id(1)
    n = pl.num_programs(1)
    c = CHUNK
    rb = PREP_ROWS

    pa = pa_ref[0]
    prev_row = jnp.where(i > 0, prev_ref[0, SUBLANES - 1:SUBLANES, :], 0.0)
    next_row = jnp.where(i < n - 1, next_ref[0, 0:1, :], 0.0)
    row = lax.broadcasted_iota(jnp.int32, (rb, 1), 0)
    prev = jnp.where(row == 0, prev_row, pltpu.roll(pa, 1, 0))
    nxt = jnp.where(row == rb - 1, next_row, pltpu.roll(pa, rb - 1, 0))
    xs = pa + mu_ref[0:1, :] * (prev - pa) + mu_ref[1:2, :] * (nxt - pa)

    aw = A_WIDTH
    r = xs[:, 0:aw]
    k = xs[:, aw:2 * aw]
    v = xs[:, 2 * aw:3 * aw]
    g = xs[:, 3 * aw:4 * aw]
    tw = jnp.tanh(xs[:, 4 * aw:4 * aw + 2 * LOW_RANK]).astype(BF16)
    alo = xs[:, 4 * aw + 2 * LOW_RANK:4 * aw + 4 * LOW_RANK].astype(BF16)

    head_ones = ones_ref[...]
    kk = k * kk_ref[...]
    kk = kk * lax.rsqrt(jnp.maximum(_head_sums(kk * kk, head_ones), 1e-24))
    ka = ka_ref[...]
    a_dir = [_sigmoid(a0_ref[d] + _dot(alo, a2_ref[d])) for d in range(2)]
    k_both = k * (2.0 + (a_dir[0] + a_dir[1] - 2.0) * ka)
    bonus = _head_sums(r * k_both * rk_ref[...], head_ones) * v
    sg = _silu(g)
    z_ref[0, 0] = sg.astype(_ACT_DTYPE)
    z_ref[0, 1] = (bonus * sg).astype(_ACT_DTYPE)
    v_ref[0] = v.astype(BF16)

    for d in range(2):
        u = w0_ref[d] + _dot(tw, w2_ref[d])
        logw = -math.exp(-0.5) * _sigmoid(u)
        sums = _dot_exact_lhs(cum_ref[d], logw)
        cl = sums[0:rb]
        total = sums[rb:2 * rb]
        e_out = jnp.exp(-cl)
        e_tail = jnp.exp(total - cl)
        k_d = k * (1.0 + (a_dir[d] - 1.0) * ka)
        b_d = kk * a_dir[d]
        ops_ref[0, d, _OP_A] = (-kk * jnp.exp(cl - logw)).astype(BF16)
        ops_ref[0, d, _OP_R] = (r * jnp.exp(cl)).astype(BF16)
        ops_ref[0, d, _OP_B] = (b_d * e_out).astype(BF16)
        ops_ref[0, d, _OP_K] = (k_d * e_out).astype(BF16)
        ops_ref[0, d, _OP_BTAIL] = (b_d * e_tail).astype(BF16)
        ops_ref[0, d, _OP_KTAIL] = (k_d * e_tail).astype(BF16)
        p_end = jnp.exp(total)
        for j in range(rb // c):
            pend_ref[0, d, j] = p_end[j * c:j * c + 1, :]


def _prep(pa, shift_mu, w0, w2p, a0, a2p, k_k, k_a, r_k):
    bn, t, _ = pa.shape
    rb = PREP_ROWS
    n = t // rb
    rows8 = rb // SUBLANES
    head_ones = _head_ones()
    cum_lhs =jnp.asarray(_cumulative_lhs(), dtype=BF16)
    const2 = lambda b, i: (0, 0)
    const3 = lambda b, i: (0, 0, 0)
    return pl.pallas_call(
        _prep_kernel,
        grid=(bn, n),
        in_specs=[
            pl.BlockSpec((1, rb, A_COLS), lambda b, i: (b, i, 0)),
            pl.BlockSpec((1, SUBLANES, A_COLS), lambda b, i: (b, jnp.maximum(i * rows8 - 1, 0), 0)),
            pl.BlockSpec((1, SUBLANES, A_COLS), lambda b, i: (b, jnp.minimum((i + 1) * rows8, t // SUBLANES - 1), 0)),
            pl.BlockSpec((2, A_COLS), const2),
            pl.BlockSpec((2, 1, A_WIDTH), const3),
            pl.BlockSpec((2, 2 * LOW_RANK, A_WIDTH), const3),
            pl.BlockSpec((2, 1, A_WIDTH), const3),
            pl.BlockSpec((2, 2 * LOW_RANK, A_WIDTH), const3),
            pl.BlockSpec((1, A_WIDTH), const2),
            pl.BlockSpec((1, A_WIDTH), const2),
            pl.BlockSpec((1, A_WIDTH), const2),
            pl.BlockSpec((A_WIDTH, A_WIDTH), const2),
            pl.BlockSpec((2, 2 * rb, rb), const3),
        ],
        out_specs=[
            pl.BlockSpec((1, 2, _N_OPS, rb, A_WIDTH), lambda b, i: (b, 0, 0, i, 0)),
            pl.BlockSpec((1, rb, A_WIDTH), lambda b, i: (b, i, 0)),
            pl.BlockSpec((1, 2, rb // CHUNK, 1, A_WIDTH), lambda b, i: (b, 0, i, 0, 0)),
            pl.BlockSpec((1, 2, rb, A_WIDTH), lambda b, i: (b, 0, i, 0)),
        ],
        out_shape=[
            jax.ShapeDtypeStruct((bn, 2, _N_OPS, t, A_WIDTH), BF16),
            jax.ShapeDtypeStruct((bn, t, A_WIDTH), BF16),
            jax.ShapeDtypeStruct((bn, 2, t // CHUNK, 1, A_WIDTH), F32),
            jax.ShapeDtypeStruct((bn, 2, t, A_WIDTH), _ACT_DTYPE),
        ],
        compiler_params=pltpu.CompilerParams(
            dimension_semantics=("parallel", "parallel"), vmem_limit_bytes=VMEM_LIMIT_BYTES),
        name="rwkv_prep",
    )(pa, pa, pa, shift_mu, w0.reshape(2, 1, A_WIDTH), w2p.astype(BF16), a0.reshape(2, 1, A_WIDTH),
      a2p.astype(BF16), k_k.reshape(1, A_WIDTH), k_a.reshape(1, A_WIDTH), r_k.reshape(1, A_WIDTH), head_ones, cum_lhs)


def _scan_kernel(ops_ref, v_ref, pend_ref, mask_ref, y_ref, h_scr, rhat_scr, y0_scr, g_scr, hadd_scr):
    d = pl.program_id(1)
    i = pl.program_id(2)
    fwd = d == 0
    c = CHUNK

    @pl.when(i == 0)
    def _():
        h_scr[...] = jnp.zeros_like(h_scr)
        rhat_scr[...] = jnp.zeros_like(rhat_scr)
        y0_scr[...] = jnp.zeros_like(y0_scr)
        g_scr[...] = jnp.zeros_like(g_scr)
        hadd_scr[...] = jnp.zeros_like(hadd_scr)

    slot_new = i % 2
    slot_old = 1 - slot_new

    probs = []
    for j in range(SCAN_SUB):
        jj = jnp.where(fwd, j, SCAN_SUB - 1 - j)
        rows = pl.ds(pl.multiple_of(jj * c, c), c)
        for s in range(N_STACKS):
            probs.append((jj, rows, s, slice(s * STACK_LANES, (s + 1) * STACK_LANES)))
    todo = range(len(probs))

    def carry(j):
        for p in range(j * N_STACKS, (j + 1) * N_STACKS):
            _, rows, s, sl = probs[p]
            h0 = h_scr[s].astype(BF16)
            y_s = _dot(rhat_scr[slot_old, p], h0) + y0_scr[slot_old, p]
            h_scr[s] = _dot(g_scr[slot_old, p], h0) + hadd_scr[slot_old, p]
            y = y_s[0:c]
            for hh in range(1, HEADS_PER_STACK):
                y = y + y_s[hh * c:(hh + 1) * c]
            y_ref[0, 0, rows, sl] = y

    n_stages = _N_LEVELS + 1
    carry_after = {(j * n_stages) // SCAN_SUB: j for j in range(SCAN_SUB)}

    def stage_done(stage):
        if stage in carry_after:
            carry(carry_after[stage])

    stage_done(0)
    lane_head = lax.broadcasted_iota(jnp.int32, (1, STACK_LANES), 1) // A_HEAD_DIM
    head_sel = [lane_head == hh for hh in range(HEADS_PER_STACK)]
    strict = mask_ref[0, _MASK_STRICT] != 0.0
    incl = mask_ref[0, _MASK_INCL] != 0.0
    eye = mask_ref[0, _MASK_EYE]

    def stack(ref_slice):
        return jnp.concatenate([jnp.where(head_sel[hh], ref_slice, jnp.zeros_like(ref_slice))
                                for hh in range(HEADS_PER_STACK)], axis=0)

    st = []
    for jj, rows, s, sl in probs:
        st.append({name: stack(ops_ref[0, 0, op, rows, sl]) for name, op in
                   (("a", _OP_A), ("r", _OP_R), ("b", _OP_B), ("k", _OP_K), ("bt", _OP_BTAIL), ("kt", _OP_KTAIL))})
        st[-1]["v"] = stack(v_ref[0, rows, sl])
    l_ab = [jnp.where(strict, _dot_nt(q["a"], q["b"]), 0.0) for q in st]
    l_ak = [jnp.where(strict, _dot_nt(q["a"], q["k"]), 0.0).astype(BF16) for q in st]
    m_rb = [jnp.where(incl, _dot_nt(q["r"], q["b"]), 0.0).astype(BF16) for q in st]
    m_rk = [jnp.where(incl, _dot_nt(q["r"], q["k"]), 0.0).astype(BF16) for q in st]
    stage_done(1)
    inv = [eye + l * mask_ref[0, _MASK_LEVEL0] for l in l_ab]
    for lv in range(1, _N_LEVELS):
        inv_b = [x.astype(BF16) for x in inv]
        lo = [(l * mask_ref[0, _MASK_LEVEL0 + lv]).astype(BF16) for l in l_ab]
        tmp = [_dot(lo[p], inv_b[p]).astype(BF16) for p in todo]
        inv = [inv[p] + _dot(inv_b[p], tmp[p]) for p in todo]
        stage_done(lv + 1)
    inv_b = [x.astype(BF16) for x in inv]
    a_hat = [_dot(inv_b[p], st[p]["a"]).astype(BF16) for p in todo]
    w = [_dot(l_ak[p], st[p]["v"]).astype(BF16) for p in todo]
    u0 = [_dot(inv_b[p], w[p]).astype(BF16) for p in todo]
    for p in todo:
        rhat_scr[slot_new, p] = (st[p]["r"].astype(F32) + _dot(m_rb[p], a_hat[p])).astype(BF16)
        y0_scr[slot_new, p] = _dot(m_rb[p], u0[p]) + _dot(m_rk[p], st[p]["v"])
        g_scr[slot_new, p] = (eye * pend_ref[0, 0, probs[p][0]][:, probs[p][3]]
                              + _dot_tn(st[p]["bt"], a_hat[p])).astype(BF16)
        hadd_scr[slot_new, p] = _dot_tn(st[p]["bt"], u0[p]) + _dot_tn(st[p]["kt"], st[p]["v"])


def _scan(ops, v, pend):
    bn, _, _, t, _ = ops.shape
    rb = SCAN_SUB * CHUNK
    n = t // rb
    n_probs = SCAN_SUB * N_STACKS
    masks = jnp.asarray(_scan_masks())

    def block_of(d, k):
        return jnp.where(d == 0, k, n - 1 - k)

    def in_block(d, i):
        return block_of(d, jnp.minimum(i, n - 1))

    def out_block(d, i):
        return block_of(d, jnp.maximum(i - 1, 0))

    mat = (2, n_probs, MXU_DIM, MXU_DIM)
    return pl.pallas_call(
        _scan_kernel,
        grid=(bn, 2, n + 1),
        in_specs=[
            pl.BlockSpec((1, 1, _N_OPS, rb, A_WIDTH), lambda b, d, i: (b, d, 0, in_block(d, i), 0)),
            pl.BlockSpec((1, rb, A_WIDTH), lambda b, d, i: (b, in_block(d, i), 0)),
            pl.BlockSpec((1, 1, SCAN_SUB, 1, A_WIDTH), lambda b, d, i: (b, d, in_block(d, i), 0, 0)),
            pl.BlockSpec((1, _N_MASKS, MXU_DIM, MXU_DIM), lambda b, d, i: (d, 0, 0, 0)),
        ],
        out_specs=pl.BlockSpec((1, 1, rb, A_WIDTH), lambda b, d, i: (b, d, out_block(d, i), 0)),
        out_shape=jax.ShapeDtypeStruct((bn, 2, t, A_WIDTH), F32),
        scratch_shapes=[
            pltpu.VMEM((N_STACKS, MXU_DIM, MXU_DIM), F32),
            pltpu.VMEM(mat, BF16),
            pltpu.VMEM(mat, F32),
            pltpu.VMEM(mat, BF16),
            pltpu.VMEM(mat, F32),
        ],
        compiler_params=pltpu.CompilerParams(
            dimension_semantics=("parallel", "arbitrary", "arbitrary"), vmem_limit_bytes=VMEM_LIMIT_BYTES),
        name="rwkv_scan",
    )(ops, v, pend, masks)


def _rwkv(pa, shift_mu, w0, w2p, a0, a2p, k_k, k_a, r_k):
    ops, v, pend, z = _prep(pa, shift_mu, w0, w2p, a0, a2p, k_k, k_a, r_k)
    return _scan(ops, v, pend), z


def _t5_bucket_np(rel):
    half_b = N_BUCKETS // 2
    ret = np.where(rel > 0, half_b, 0)
    n = np.abs(rel)
    large = MAX_EXACT + (np.log(np.maximum(n, 1).astype(np.float32) / MAX_EXACT)
                         / math.log(MAX_DISTANCE / MAX_EXACT) * (half_b - MAX_EXACT)).astype(np.int32)
    large = np.minimum(large, half_b - 1)
    return ret + np.where(n < MAX_EXACT, n, large)


def _band_buckets():
    i = np.arange(Q_BLOCK)[:, None]
    j = np.arange(2 * Q_BLOCK)[None, :]
    rel = j - HALF - i
    out = []
    for _, dil in B_GROUPS:
        bkt = _t5_bucket_np(rel * dil)
        out.append(np.where(np.abs(rel) <= HALF, bkt, -1))
    return np.stack(out).astype(np.int32)


def _attn_kernel(tbl_ref, bkt_ref, q_ref, k_ref, v_ref, o_ref, m_scr, l_scr, acc_scr, bias_scr, *, t):
    h4 = pl.program_id(1)
    gi = pl.program_id(2)
    head = gi * B_HPG + h4
    qb = Q_BLOCK
    scale = B_HEAD_DIM ** -0.5

    @pl.when(gi == 0)
    def _():
        m_scr[...] = jnp.full_like(m_scr, NEG)
        l_scr[...] = jnp.zeros_like(l_scr)
        acc_scr[...] = jnp.zeros_like(acc_scr)

    bkt = bkt_ref[0]
    bias = jnp.full((qb, 2 * qb), NEG, F32)
    for bb in range(N_BUCKETS):
        bias = jnp.where(bkt == bb, tbl_ref[bb, head], bias)
    bias_scr[...] = bias
    col = lax.broadcasted_iota(jnp.int32, (1, 2 * qb), 1)

    def run(dil):
        sub_len = t // dil
        nb = sub_len // qb

        def body(it, carry):
            todo = range(ATTN_BLOCKS_PER_ITER)
            q_rows, qs, kws, vws, valid = [], [], [], [], []
            for u in todo:
                blk = it * ATTN_BLOCKS_PER_ITER + u
                rho = blk // nb
                bq = blk % nb
                rows = pl.ds(rho + dil * qb * bq, qb, stride=dil)
                s0 = jnp.maximum(qb * bq - HALF, 0)
                s3 = jnp.minimum(qb * bq + qb, sub_len - HALF)
                rows0 = pl.ds(rho + dil * s0, HALF, stride=dil)
                rows3 = pl.ds(rho + dil * s3, HALF, stride=dil)

                def window(ref):
                    return jnp.concatenate([ref[0, 0, rows0, :], ref[0, 0, rows, :], ref[0, 0, rows3, :]], axis=0)

                q_rows.append(rows)
                qs.append(q_ref[0, 0, rows, :].astype(_MXU_DTYPE))
                kws.append(window(k_ref).astype(_MXU_DTYPE))
                vws.append(window(v_ref).astype(_MXU_DTYPE))
                pos = qb * bq - HALF + col
                valid.append((pos >= 0) & (pos < sub_len))
            bias_b = bias_scr[...]
            s = [jnp.where(valid[u], _dot_nt(qs[u], kws[u]) * scale + bias_b, NEG) for u in todo]
            m_blk = [jnp.max(s[u], axis=-1, keepdims=True) for u in todo]
            m_old = [m_scr[q_rows[u], :] for u in todo]
            l_old = [l_scr[q_rows[u], :] for u in todo]
            acc_old = [acc_scr[q_rows[u], :] for u in todo]
            m_new = [jnp.maximum(m_old[u], m_blk[u]) for u in todo]
            alpha = [jnp.exp(m_old[u] - m_new[u]) for u in todo]
            p = [jnp.exp(s[u] - m_new[u][:, 0:1]) for u in todo]
            l_new = [alpha[u] * l_old[u] + jnp.sum(p[u], axis=-1, keepdims=True) for u in todo]
            acc = [alpha[u] * acc_old[u] + _dot(p[u].astype(_MXU_DTYPE), vws[u]) for u in todo]
            for u in todo:
                m_scr[q_rows[u], :] = m_new[u]
                l_scr[q_rows[u], :] = l_new[u]
                acc_scr[q_rows[u], :] = acc[u]
            return carry

        lax.fori_loop(0, dil * nb // ATTN_BLOCKS_PER_ITER, body, 0)

    for gidx, (_, dil) in enumerate(B_GROUPS):
        pl.when(gi == gidx)(functools.partial(run, dil))

    @pl.when(gi == N_GROUPS - 1)
    def _():
        o_ref[0] = (acc_scr[...] / l_scr[...]).astype(o_ref.dtype)


def _attention(rel_bias, buckets, q, k, v):
    bn, _, t, _ = q.shape
    qkv_spec = pl.BlockSpec((1, 1, t, B_HEAD_DIM), lambda b, h, g: (b, g * B_HPG + h, 0, 0))
    return pl.pallas_call(
        functools.partial(_attn_kernel, t=t),
        grid=(bn, B_HPG, N_GROUPS),
        in_specs=[
            pl.BlockSpec(memory_space=pltpu.SMEM),
            pl.BlockSpec((1, Q_BLOCK, 2 * Q_BLOCK), lambda b, h, g: (g, 0, 0)),
            qkv_spec, qkv_spec, qkv_spec,
        ],
        out_specs=pl.BlockSpec((1, t, B_HEAD_DIM), lambda b, h, g: (b, 0, h)),
        out_shape=jax.ShapeDtypeStruct((bn, t, B_WIDTH), _ACT_DTYPE),
        scratch_shapes=[
            pltpu.VMEM((t, B_HEAD_DIM), F32),
            pltpu.VMEM((t, B_HEAD_DIM), F32),
            pltpu.VMEM((t, B_HEAD_DIM), F32),
            pltpu.VMEM((Q_BLOCK, 2 * Q_BLOCK), F32),
        ],
        compiler_params=pltpu.CompilerParams(
            dimension_semantics=("parallel", "parallel", "arbitrary"), vmem_limit_bytes=VMEM_LIMIT_BYTES),
        name="attention",
    )(rel_bias, buckets, q, k, v)


def _merge_kernel(x_ref, mod_ref, y_ref, z_ref, o_ref, pg_ref, lw_ref, lb_ref, ones_ref, wa_ref, wb_ref, wo_ref,
                  out_ref):
    y =y_ref[0, 0] + y_ref[0, 1]
    head_ones = ones_ref[...]
    mean = _head_sums(y, head_ones) * (1.0 / A_HEAD_DIM)
    yc = y - mean
    var = _head_sums(yc * yc, head_ones) * (1.0 / A_HEAD_DIM)
    yn = yc * lax.rsqrt(var + GN_EPS) * lw_ref[...] + lb_ref[...]
    ya = yn * z_ref[0, 0].astype(F32) + z_ref[0, 1].astype(F32)
    pg = pg_ref[0].astype(F32)
    yb = o_ref[0].astype(F32) * _silu(pg[:, 0:B_WIDTH])
    gate_a = _sigmoid(pg[:, B_WIDTH:B_WIDTH + D_MODEL])
    gate_b = _sigmoid(pg[:, B_WIDTH + D_MODEL:])
    merged = (gate_a * _dot(ya.astype(_MXU_DTYPE), wa_ref[...])
              + gate_b * _dot(yb.astype(_MXU_DTYPE), wb_ref[...]))
    out_ref[0] = x_ref[0] + mod_ref[0, 2:3, :] * _dot(merged.astype(_MXU_DTYPE), wo_ref[...])


def _merge(x, mod3, y, z, o, pg, lnx_w, lnx_b, wa_b, wb_b, wo_b):
    bn, t, _ = x.shape
    tm = MERGE_ROWS
    const2 = lambda b, i: (0, 0)
    return pl.pallas_call(
        _merge_kernel,
        grid=(bn, t // tm),
        in_specs=[
            pl.BlockSpec((1, tm, D_MODEL), lambda b, i: (b, i, 0)),
            pl.BlockSpec((1, 3, D_MODEL), lambda b, i: (b, 0, 0)),
            pl.BlockSpec((1, 2, tm, A_WIDTH), lambda b, i: (b, 0, i, 0)),
            pl.BlockSpec((1, 2, tm, A_WIDTH), lambda b, i: (b, 0, i, 0)),
            pl.BlockSpec((1, tm, B_WIDTH), lambda b, i: (b, i, 0)),
            pl.BlockSpec((1, tm, PG_COLS), lambda b, i: (b, i, 0)),
            pl.BlockSpec((1, A_WIDTH), const2),
            pl.BlockSpec((1, A_WIDTH), const2),
            pl.BlockSpec((A_WIDTH, A_WIDTH), const2),
            pl.BlockSpec((A_WIDTH, D_MODEL), const2),
            pl.BlockSpec((B_WIDTH, D_MODEL), const2),
            pl.BlockSpec((D_MODEL, D_MODEL), const2),
        ],
        out_specs=pl.BlockSpec((1, tm, D_MODEL), lambda b, i: (b, i, 0)),
        out_shape=jax.ShapeDtypeStruct((bn, t, D_MODEL), x.dtype),
        compiler_params=pltpu.CompilerParams(
            dimension_semantics=("parallel", "parallel"), vmem_limit_bytes=VMEM_LIMIT_BYTES),
        name="merge",
    )(x, mod3, y, z, o, pg, lnx_w.reshape(1, A_WIDTH), lnx_b.reshape(1, A_WIDTH), _head_ones(), wa_b, wb_b,
      wo_b)


def _pad_low_rank(w):
    z = jnp.zeros((LOW_RANK, A_WIDTH), w.dtype)
    return jnp.stack([jnp.concatenate([w[0], z], axis=0), jnp.concatenate([z, w[1]], axis=0)])


def kernel(x_prompt, x_sample, c_prompt, c_sample, rel_bias, norm_w, w_ada, b_ada, w_in, shift_mu, w0, w2, a0, a2,
           k_k, k_a, r_k, lnx_w, lnx_b, q_gain, k_gain, w_out_a, w_out_b, w_out):
    depth = norm_w.shape[0]
    buckets = jnp.asarray(_band_buckets())

    def trunk(x, c):
        bn, t, _ = x.shape
        assert t % (Q_BLOCK * B_GROUPS[-1][1]) == 0 and t % MERGE_ROWS == 0
        for l in range(depth):
            mod3 = _modulation(c, w_ada[l], b_ada[l]).reshape(bn, 3, D_MODEL)
            pa, q, k, v, pg = _projection(x, mod3, norm_w[l], w_in[l].astype(_MXU_DTYPE), q_gain[l], k_gain[l])
            y, z = _rwkv(pa, shift_mu[l], w0[l], _pad_low_rank(w2[l]), a0[l], _pad_low_rank(a2[l]),
                         k_k[l], k_a[l], r_k[l].reshape(A_WIDTH))
            o = _attention(rel_bias, buckets, q, k, v)
            x = _merge(x, mod3, y, z, o, pg, lnx_w[l], lnx_b[l], w_out_a[l].astype(_MXU_DTYPE),
                       w_out_b[l].astype(_MXU_DTYPE), w_out[l].astype(_MXU_DTYPE))
        return x

    return (trunk(x_prompt, c_prompt), trunk(x_sample, c_sample))
```

```python
import functools
import math

import numpy as np
import jax
import jax.numpy as jnp
from jax import lax
from jax.experimental import pallas as pl
from jax.experimental.pallas import tpu as pltpu

F32 = jnp.float32
BF16 = jnp.bfloat16
HIGHEST = lax.Precision.HIGHEST

D_MODEL = 1024
A_HEADS = 8
A_HEAD_DIM = 64
A_WIDTH = A_HEADS * A_HEAD_DIM
LOW_RANK = 64
A_COLS = 4 * A_WIDTH + 4 * LOW_RANK
GN_EPS = 64e-5
B_GROUPS = ((128, 1), (512, 4), (2048, 16))
N_GROUPS = len(B_GROUPS)
B_HPG = 4
B_HEAD_DIM = 128
B_HEADS = N_GROUPS * B_HPG
B_QKV = B_HEADS * B_HEAD_DIM
B_WIDTH = B_HPG * B_HEAD_DIM
Q_BLOCK = 128
HALF = 64
N_BUCKETS = 32
MAX_EXACT = 8
MAX_DISTANCE = 1024
PG_COLS = B_WIDTH + 2 * D_MODEL
IN_COLS = A_COLS + 3 * B_QKV + PG_COLS
RMS_EPS = 1e-6
NEG = -1e30

LANES = 128
SUBLANES = 8
BF16_SUBLANES = 16
MXU_DIM = 256
VMEM_LIMIT_BYTES = 60000 * 1024

PROJ_ROWS = 256
CHUNK = 64
PREP_ROWS = 256
SCAN_SUB = 4
HEADS_PER_STACK = MXU_DIM // CHUNK
STACK_LANES = HEADS_PER_STACK * A_HEAD_DIM
N_STACKS = A_HEADS // HEADS_PER_STACK
MERGE_ROWS = 512
ATTN_BLOCKS_PER_ITER = 4

_MXU_DTYPE = BF16
_ACT_DTYPE = BF16


def _dot(a, b, precision=None):
    return jnp.dot(a, b, preferred_element_type=F32, precision=precision)


def _dot_nt(a, b, precision=None):
    return lax.dot_general(a, b, (((1,), (1,)), ((), ())), preferred_element_type=F32, precision=precision)


def _dot_tn(a, b, precision=None):
    return lax.dot_general(a, b, (((0,), (0,)), ((), ())), preferred_element_type=F32, precision=precision)


def _sigmoid(z):
    return 1.0 / (1.0 + jnp.exp(-z))


def _silu(z):
    return z * _sigmoid(z)


def _modulation_kernel(c_ref, w_ref, b_ref, o_ref):
    o_ref[...] = _dot(_silu(c_ref[...]), w_ref[...], HIGHEST) + b_ref[...]


def _modulation(c, w_ada, b_ada):
    bn = c.shape[0]
    return pl.pallas_call(
        _modulation_kernel,
        out_shape=jax.ShapeDtypeStruct((bn, 3 * D_MODEL), F32),
        compiler_params=pltpu.CompilerParams(vmem_limit_bytes=VMEM_LIMIT_BYTES),
        name="modulation",
    )(c, w_ada, b_ada.reshape(1, 3 * D_MODEL))


def _proj_kernel(x_ref, mod_ref, nw_ref, w_ref, qg_ref, kg_ref, pa_ref, q_ref, k_ref, v_ref, pg_ref):
    x = x_ref[0]
    ms = jnp.mean(x * x, axis=-1, keepdims=True)
    h = x * lax.rsqrt(ms + RMS_EPS) * nw_ref[...] * (1.0 + mod_ref[0, 1:2, :]) + mod_ref[0, 0:1, :]
    hb = h.astype(_MXU_DTYPE)
    step = 2 * MXU_DIM

    def cols(c0, width):
        return _dot(hb, w_ref[:, c0:c0 + width])

    for c0 in range(0, A_COLS, step):
        width = min(step, A_COLS - c0)
        pa_ref[0, :, c0:c0 + width] = cols(c0, width)

    def head_norm(z, gain):
        return z * lax.rsqrt(jnp.mean(z * z, axis=-1, keepdims=True) + RMS_EPS) * gain

    heads_per_step = step // B_HEAD_DIM
    for c0 in range(0, B_QKV, step):
        qc = cols(A_COLS + c0, step)
        kc = cols(A_COLS + B_QKV + c0, step)
        vc = cols(A_COLS + 2 * B_QKV + c0, step)
        for j in range(heads_per_step):
            hd = c0 // B_HEAD_DIM + j
            sl = slice(j * B_HEAD_DIM, (j + 1) * B_HEAD_DIM)
            q_ref[0, hd] = head_norm(qc[:, sl], qg_ref[...])
            k_ref[0, hd] = head_norm(kc[:, sl], kg_ref[...])
            v_ref[0, hd] = vc[:, sl]

    pg0 = A_COLS + 3 * B_QKV
    for c0 in range(0, PG_COLS, step):
        pg_ref[0, :, c0:c0 + step] = cols(pg0 + c0, step).astype(_ACT_DTYPE)


def _projection(x, mod3, norm_w, w_in_b, q_gain, k_gain):
    bn, t, _ = x.shape
    tm = PROJ_ROWS
    grid = (bn, t // tm)
    const2 = lambda b, i: (0, 0)
    return pl.pallas_call(
        _proj_kernel,
        grid=grid,
        in_specs=[
            pl.BlockSpec((1, tm, D_MODEL), lambda b, i: (b, i, 0)),
            pl.BlockSpec((1, 3, D_MODEL), lambda b, i: (b, 0, 0)),
            pl.BlockSpec((1, D_MODEL), const2),
            pl.BlockSpec((D_MODEL, IN_COLS), const2, pipeline_mode=pl.Buffered(1)),
            pl.BlockSpec((1, B_HEAD_DIM), const2),
            pl.BlockSpec((1, B_HEAD_DIM), const2),
        ],
        out_specs=[
            pl.BlockSpec((1, tm, A_COLS), lambda b, i: (b, i, 0)),
            pl.BlockSpec((1, B_HEADS, tm, B_HEAD_DIM), lambda b, i: (b, 0, i, 0)),
            pl.BlockSpec((1, B_HEADS, tm, B_HEAD_DIM), lambda b, i: (b, 0, i, 0)),
            pl.BlockSpec((1, B_HEADS, tm, B_HEAD_DIM), lambda b, i: (b, 0, i, 0)),
            pl.BlockSpec((1, tm, PG_COLS), lambda b, i: (b, i, 0)),
        ],
        out_shape=[
            jax.ShapeDtypeStruct((bn, t, A_COLS), F32),
            jax.ShapeDtypeStruct((bn, B_HEADS, t, B_HEAD_DIM), F32),
            jax.ShapeDtypeStruct((bn, B_HEADS, t, B_HEAD_DIM), F32),
            jax.ShapeDtypeStruct((bn, B_HEADS, t, B_HEAD_DIM), F32),
            jax.ShapeDtypeStruct((bn, t, PG_COLS), _ACT_DTYPE),
        ],
        compiler_params=pltpu.CompilerParams(
            dimension_semantics=("parallel", "parallel"), vmem_limit_bytes=VMEM_LIMIT_BYTES),
        name="projection",
    )(x, mod3, norm_w.reshape(1, D_MODEL), w_in_b, q_gain.reshape(1, B_HEAD_DIM), k_gain.reshape(1, B_HEAD_DIM))


def _split3(z):
    hi = z.astype(BF16)
    r1 = z - hi.astype(F32)
    mid = r1.astype(BF16)
    lo = (r1 - mid.astype(F32)).astype(BF16)
    return hi, mid, lo


def _dot_exact_lhs(lhs_b, z):
    hi, mid, lo = _split3(z)
    return _dot(lhs_b, hi) + (_dot(lhs_b, mid) + _dot(lhs_b, lo))


def _head_sums(z, head_ones_b):
    hi = z.astype(BF16)
    lo = (z - hi.astype(F32)).astype(BF16)
    return _dot(hi, head_ones_b) + _dot(lo, head_ones_b)


def _head_ones():
    head_of = np.arange(A_WIDTH) // A_HEAD_DIM
    return jnp.asarray(head_of[:, None] == head_of[None, :], dtype=BF16)


_MASK_STRICT, _MASK_INCL, _MASK_EYE, _MASK_LEVEL0 = 0, 1, 2, 3
_N_LEVELS = int(math.log2(CHUNK))
_N_MASKS = _MASK_LEVEL0 + _N_LEVELS
_OP_A, _OP_R, _OP_B, _OP_K, _OP_BTAIL, _OP_KTAIL = range(6)
_N_OPS = 6


def _scan_masks():
    idx = np.arange(MXU_DIM)
    out = []
    for d in range(2):
        p = idx % CHUNK if d == 0 else CHUNK - 1 - idx % CHUNK
        pi, pj = p[:, None], p[None, :]
        masks = [pj < pi, pj <= pi, idx[:, None] == idx[None, :]]
        for lv in range(_N_LEVELS):
            sz = 1 << lv
            masks.append(((pi // sz) % 2 == 1) & (pj // sz == pi // sz - 1))
        out.append(np.stack(masks))
    return np.stack(out).astype(np.float32)


def _cumulative_lhs():
    idx = np.arange(PREP_ROWS)
    same = idx[:, None] // CHUNK == idx[None, :] // CHUNK
    fwd = same & (idx[None, :] <= idx[:, None])
    bwd = same & (idx[None, :] >= idx[:, None])
    return np.stack([np.concatenate([fwd, same]), np.concatenate([bwd, same])]).astype(np.float32)


def _prep_kernel(pa_ref, prev_ref, next_ref, mu_ref, w0_ref, w2_ref, a0_ref, a2_ref, kk_ref, ka_ref, rk_ref, ones_ref,
                 cum_ref, ops_ref, v_ref, pend_ref, z_ref):
    i = pl.program_id(1)
    n = pl.num_programs(1)
    c = CHUNK
    rb = PREP_ROWS

    pa = pa_ref[0]
    prev_row = jnp.where(i > 0, prev_ref[0, SUBLANES - 1:SUBLANES, :], 0.0)
    next_row = jnp.where(i < n - 1, next_ref[0, 0:1, :], 0.0)
    row = lax.broadcasted_iota(jnp.int32, (rb, 1), 0)
    prev = jnp.where(row == 0, prev_row, pltpu.roll(pa, 1, 0))
    nxt = jnp.where(row == rb - 1, next_row, pltpu.roll(pa, rb - 1, 0))
    xs = pa + mu_ref[0:1, :] * (prev - pa) + mu_ref[1:2, :] * (nxt - pa)

    aw = A_WIDTH
    r = xs[:, 0:aw]
    k = xs[:, aw:2 * aw]
    v = xs[:, 2 * aw:3 * aw]
    g = xs[:, 3 * aw:4 * aw]
    tw = jnp.tanh(xs[:, 4 * aw:4 * aw + 2 * LOW_RANK]).astype(BF16)
    alo = xs[:, 4 * aw + 2 * LOW_RANK:4 * aw + 4 * LOW_RANK].astype(BF16)

    head_ones = ones_ref[...]
    kk = k * kk_ref[...]
    kk = kk * lax.rsqrt(jnp.maximum(_head_sums(kk * kk, head_ones), 1e-24))
    ka = ka_ref[...]
    a_dir = [_sigmoid(a0_ref[d] + _dot(alo, a2_ref[d])) for d in range(2)]
    k_both = k * (2.0 + (a_dir[0] + a_dir[1] - 2.0) * ka)
    bonus = _head_sums(r * k_both * rk_ref[...], head_ones) * v
    sg = _silu(g)
    z_ref[0, 0] = sg.astype(_ACT_DTYPE)
    z_ref[0, 1] = (bonus * sg).astype(_ACT_DTYPE)
    v_ref[0] = v.astype(BF16)

    for d in range(2):
        u = w0_ref[d] + _dot(tw, w2_ref[d])
        logw = -math.exp(-0.5) * _sigmoid(u)
        sums = _dot_exact_lhs(cum_ref[d], logw)
        cl = sums[0:rb]
        total = sums[rb:2 * rb]
        e_out = jnp.exp(-cl)
        e_tail = jnp.exp(total - cl)
        k_d = k * (1.0 + (a_dir[d] - 1.0) * ka)
        b_d = kk * a_dir[d]
        ops_ref[0, d, _OP_A] = (-kk * jnp.exp(cl - logw)).astype(BF16)
        ops_ref[0, d, _OP_R] = (r * jnp.exp(cl)).astype(BF16)
        ops_ref[0, d, _OP_B] = (b_d * e_out).astype(BF16)
        ops_ref[0, d, _OP_K] = (k_d * e_out).astype(BF16)
        ops_ref[0, d, _OP_BTAIL] = (b_d * e_tail).astype(BF16)
        ops_ref[0, d, _OP_KTAIL] = (k_d * e_tail).astype(BF16)
        p_end = jnp.exp(total)
        for j in range(rb // c):
            pend_ref[0, d, j] = p_end[j * c:j * c + 1, :]


def _prep(pa, shift_mu, w0, w2p, a0, a2p, k_k, k_a, r_k):
    bn, t, _ = pa.shape
    rb = PREP_ROWS
    n = t // rb
    rows8 = rb // SUBLANES
    head_ones = _head_ones()
    cum_lhs =jnp.asarray(_cumulative_lhs(), dtype=BF16)
    const2 = lambda b, i: (0, 0)
    const3 = lambda b, i: (0, 0, 0)
    return pl.pallas_call(
        _prep_kernel,
        grid=(bn, n),
        in_specs=[
            pl.BlockSpec((1, rb, A_COLS), lambda b, i: (b, i, 0)),
            pl.BlockSpec((1, SUBLANES, A_COLS), lambda b, i: (b, jnp.maximum(i * rows8 - 1, 0), 0)),
            pl.BlockSpec((1, SUBLANES, A_COLS), lambda b, i: (b, jnp.minimum((i + 1) * rows8, t // SUBLANES - 1), 0)),
            pl.BlockSpec((2, A_COLS), const2),
            pl.BlockSpec((2, 1, A_WIDTH), const3),
            pl.BlockSpec((2, 2 * LOW_RANK, A_WIDTH), const3),
            pl.BlockSpec((2, 1, A_WIDTH), const3),
            pl.BlockSpec((2, 2 * LOW_RANK, A_WIDTH), const3),
            pl.BlockSpec((1, A_WIDTH), const2),
            pl.BlockSpec((1, A_WIDTH), const2),
            pl.BlockSpec((1, A_WIDTH), const2),
            pl.BlockSpec((A_WIDTH, A_WIDTH), const2),
            pl.BlockSpec((2, 2 * rb, rb), const3),
        ],
        out_specs=[
            pl.BlockSpec((1, 2, _N_OPS, rb, A_WIDTH), lambda b, i: (b, 0, 0, i, 0)),
            pl.BlockSpec((1, rb, A_WIDTH), lambda b, i: (b, i, 0)),
            pl.BlockSpec((1, 2, rb // CHUNK, 1, A_WIDTH), lambda b, i: (b, 0, i, 0, 0)),
            pl.BlockSpec((1, 2, rb, A_WIDTH), lambda b, i: (b, 0, i, 0)),
        ],
        out_shape=[
            jax.ShapeDtypeStruct((bn, 2, _N_OPS, t, A_WIDTH), BF16),
            jax.ShapeDtypeStruct((bn, t, A_WIDTH), BF16),
            jax.ShapeDtypeStruct((bn, 2, t // CHUNK, 1, A_WIDTH), F32),
            jax.ShapeDtypeStruct((bn, 2, t, A_WIDTH), _ACT_DTYPE),
        ],
        compiler_params=pltpu.CompilerParams(
            dimension_semantics=("parallel", "parallel"), vmem_limit_bytes=VMEM_LIMIT_BYTES),
        name="rwkv_prep",
    )(pa, pa, pa, shift_mu, w0.reshape(2, 1, A_WIDTH), w2p.astype(BF16), a0.reshape(2, 1, A_WIDTH),
      a2p.astype(BF16), k_k.reshape(1, A_WIDTH), k_a.reshape(1, A_WIDTH), r_k.reshape(1, A_WIDTH), head_ones, cum_lhs)


def _scan_kernel(ops_ref, v_ref, pend_ref, mask_ref, y_ref, h_scr, rhat_scr, y0_scr, g_scr, hadd_scr):
    @pl.when(pl.program_id(2) == 0)
    def _():
        h_scr[...] = jnp.zeros_like(h_scr)
        rhat_scr[...] = jnp.zeros_like(rhat_scr)
        y0_scr[...] = jnp.zeros_like(y0_scr)
        g_scr[...] = jnp.zeros_like(g_scr)
        hadd_scr[...] = jnp.zeros_like(hadd_scr)

    for dd in range(2):
        pl.when(pl.program_id(1) == dd)(functools.partial(
            _scan_direction, dd, ops_ref, v_ref, pend_ref, mask_ref, y_ref, h_scr, rhat_scr, y0_scr, g_scr, hadd_scr))


def _scan_direction(dd, ops_ref, v_ref, pend_ref, mask_ref, y_ref, h_scr, rhat_scr, y0_scr, g_scr, hadd_scr):
    i = pl.program_id(2)
    c = CHUNK
    slot_new = i % 2
    slot_old = 1 - slot_new

    def odd_blocks(sz):
        per_chunk = c // sz
        out = []
        for blk in range(MXU_DIM // sz):
            order = blk % per_chunk if dd == 0 else per_chunk - 1 - blk % per_chunk
            out.append((blk * sz, order % 2 == 1))
        return out

    def take_odd(z, sz):
        return jnp.concatenate([z[s0:s0 + sz] for s0, odd in odd_blocks(sz) if odd], axis=0)

    def put_odd(z_odd, sz, base=None):
        parts, taken = [], 0
        for s0, odd in odd_blocks(sz):
            if odd:
                piece = z_odd[taken * sz:(taken + 1) * sz]
                taken += 1
                parts.append(piece if base is None else base[s0:s0 + sz] + piece)
            else:
                parts.append(jnp.zeros((sz, z_odd.shape[1]), z_odd.dtype) if base is None else base[s0:s0 + sz])
        return jnp.concatenate(parts, axis=0)

    probs = []
    for j in range(SCAN_SUB):
        jj = j if dd == 0 else SCAN_SUB - 1 - j
        rows = slice(jj * c, (jj + 1) * c)
        for s in range(N_STACKS):
            probs.append((jj, rows, s, slice(s * STACK_LANES, (s + 1) * STACK_LANES)))
    todo = range(len(probs))

    def carry(j):
        for p in range(j * N_STACKS, (j + 1) * N_STACKS):
            _, rows, s, sl = probs[p]
            h0 = h_scr[s].astype(BF16)
            y_s = _dot(rhat_scr[slot_old, p], h0) + y0_scr[slot_old, p]
            h_scr[s] = _dot(g_scr[slot_old, p], h0) + hadd_scr[slot_old, p]
            y = y_s[0:c]
            for hh in range(1, HEADS_PER_STACK):
                y = y + y_s[hh * c:(hh + 1) * c]
            y_ref[0, 0, rows, sl] = y

    n_stages = _N_LEVELS + 1
    carry_after = {(j * n_stages) // SCAN_SUB: j for j in range(SCAN_SUB)}

    def stage_done(stage):
        if stage in carry_after:
            carry(carry_after[stage])

    stage_done(0)
    lane_head = lax.broadcasted_iota(jnp.int32, (1, STACK_LANES), 1) // A_HEAD_DIM
    head_sel = [lane_head == hh for hh in range(HEADS_PER_STACK)]
    strict = mask_ref[0, _MASK_STRICT] != 0.0
    incl = mask_ref[0, _MASK_INCL] != 0.0
    eye = mask_ref[0, _MASK_EYE]

    def stack(ref_slice):
        return jnp.concatenate([jnp.where(head_sel[hh], ref_slice, jnp.zeros_like(ref_slice))
                                for hh in range(HEADS_PER_STACK)], axis=0)

    st = []
    for jj, rows, s, sl in probs:
        st.append({name: stack(ops_ref[0, 0, op, rows, sl]) for name, op in
                   (("a", _OP_A), ("r", _OP_R), ("b", _OP_B), ("k", _OP_K), ("bt", _OP_BTAIL), ("kt", _OP_KTAIL))})
        st[-1]["v"] = stack(v_ref[0, rows, sl])
    l_ab = [jnp.where(strict, _dot_nt(q["a"], q["b"]), 0.0) for q in st]
    l_ak = [jnp.where(strict, _dot_nt(q["a"], q["k"]), 0.0).astype(BF16) for q in st]
    m_rb = [jnp.where(incl, _dot_nt(q["r"], q["b"]), 0.0).astype(BF16) for q in st]
    m_rk = [jnp.where(incl, _dot_nt(q["r"], q["k"]), 0.0).astype(BF16) for q in st]
    stage_done(1)
    inv = [eye + l * mask_ref[0, _MASK_LEVEL0] for l in l_ab]
    for lv in range(1, _N_LEVELS):
        sz = 1 << lv
        inv_b = [x.astype(BF16) for x in inv]
        if sz >= BF16_SUBLANES:
            lvl = take_odd(mask_ref[0, _MASK_LEVEL0 + lv], sz)
            lo = [(take_odd(l, sz) * lvl).astype(BF16) for l in l_ab]
            tmp = [put_odd(_dot(lo[p], inv_b[p]).astype(BF16), sz) for p in todo]
            inv = [put_odd(_dot(take_odd(inv_b[p], sz), tmp[p]), sz, inv[p]) for p in todo]
        else:
            lo = [(l * mask_ref[0, _MASK_LEVEL0 + lv]).astype(BF16) for l in l_ab]
            tmp = [_dot(lo[p], inv_b[p]).astype(BF16) for p in todo]
            inv = [inv[p] + _dot(inv_b[p], tmp[p]) for p in todo]
        stage_done(lv + 1)
    inv_b = [x.astype(BF16) for x in inv]
    a_hat = [_dot(inv_b[p], st[p]["a"]).astype(BF16) for p in todo]
    w = [_dot(l_ak[p], st[p]["v"]).astype(BF16) for p in todo]
    u0 = [_dot(inv_b[p], w[p]).astype(BF16) for p in todo]
    for p in todo:
        rhat_scr[slot_new, p] = (st[p]["r"].astype(F32) + _dot(m_rb[p], a_hat[p])).astype(BF16)
        y0_scr[slot_new, p] = _dot(m_rb[p], u0[p]) + _dot(m_rk[p], st[p]["v"])
        g_scr[slot_new, p] = (eye * pend_ref[0, 0, probs[p][0]][:, probs[p][3]]
                              + _dot_tn(st[p]["bt"], a_hat[p])).astype(BF16)
        hadd_scr[slot_new, p] = _dot_tn(st[p]["bt"], u0[p]) + _dot_tn(st[p]["kt"], st[p]["v"])


def _scan(ops, v, pend):
    bn, _, _, t, _ = ops.shape
    rb = SCAN_SUB * CHUNK
    n = t // rb
    n_probs = SCAN_SUB * N_STACKS
    masks = jnp.asarray(_scan_masks())

    def block_of(d, k):
        return jnp.where(d == 0, k, n - 1 - k)

    def in_block(d, i):
        return block_of(d, jnp.minimum(i, n - 1))

    def out_block(d, i):
        return block_of(d, jnp.maximum(i - 1, 0))

    mat = (2, n_probs, MXU_DIM, MXU_DIM)
    return pl.pallas_call(
        _scan_kernel,
        grid=(bn, 2, n + 1),
        in_specs=[
            pl.BlockSpec((1, 1, _N_OPS, rb, A_WIDTH), lambda b, d, i: (b, d, 0, in_block(d, i), 0)),
            pl.BlockSpec((1, rb, A_WIDTH), lambda b, d, i: (b, in_block(d, i), 0)),
            pl.BlockSpec((1, 1, SCAN_SUB, 1, A_WIDTH), lambda b, d, i: (b, d, in_block(d, i), 0, 0)),
            pl.BlockSpec((1, _N_MASKS, MXU_DIM, MXU_DIM), lambda b, d, i: (d, 0, 0, 0)),
        ],
        out_specs=pl.BlockSpec((1, 1, rb, A_WIDTH), lambda b, d, i: (b, d, out_block(d, i), 0)),
        out_shape=jax.ShapeDtypeStruct((bn, 2, t, A_WIDTH), F32),
        scratch_shapes=[
            pltpu.VMEM((N_STACKS, MXU_DIM, MXU_DIM), F32),
            pltpu.VMEM(mat, BF16),
            pltpu.VMEM(mat, F32),
            pltpu.VMEM(mat, BF16),
            pltpu.VMEM(mat, F32),
        ],
        compiler_params=pltpu.CompilerParams(
            dimension_semantics=("parallel", "arbitrary", "arbitrary"), vmem_limit_bytes=VMEM_LIMIT_BYTES),
        name="rwkv_scan",
    )(ops, v, pend, masks)


def _rwkv(pa, shift_mu, w0, w2p, a0, a2p, k_k, k_a, r_k):
    ops, v, pend, z = _prep(pa, shift_mu, w0, w2p, a0, a2p, k_k, k_a, r_k)
    return _scan(ops, v, pend), z


def _t5_bucket_np(rel):
    half_b = N_BUCKETS // 2
    ret = np.where(rel > 0, half_b, 0)
    n = np.abs(rel)
    large = MAX_EXACT + (np.log(np.maximum(n, 1).astype(np.float32) / MAX_EXACT)
                         / math.log(MAX_DISTANCE / MAX_EXACT) * (half_b - MAX_EXACT)).astype(np.int32)
    large = np.minimum(large, half_b - 1)
    return ret + np.where(n < MAX_EXACT, n, large)


def _band_buckets():
    i = np.arange(Q_BLOCK)[:, None]
    j = np.arange(2 * Q_BLOCK)[None, :]
    rel = j - HALF - i
    out = []
    for _, dil in B_GROUPS:
        bkt = _t5_bucket_np(rel * dil)
        out.append(np.where(np.abs(rel) <= HALF, bkt, -1))
    return np.stack(out).astype(np.int32)


def _attn_kernel(tbl_ref, bkt_ref, q_ref, k_ref, v_ref, o_ref, m_scr, l_scr, acc_scr, bias_scr, *, t):
    h4 = pl.program_id(1)
    gi = pl.program_id(2)
    head = gi * B_HPG + h4
    qb = Q_BLOCK
    scale = B_HEAD_DIM ** -0.5

    @pl.when(gi == 0)
    def _():
        m_scr[...] = jnp.full_like(m_scr, NEG)
        l_scr[...] = jnp.zeros_like(l_scr)
        acc_scr[...] = jnp.zeros_like(acc_scr)

    bkt = bkt_ref[0]
    bias = jnp.full((qb, 2 * qb), NEG, F32)
    for bb in range(N_BUCKETS):
        bias = jnp.where(bkt == bb, tbl_ref[bb, head], bias)
    bias_scr[...] = bias
    col = lax.broadcasted_iota(jnp.int32, (1, 2 * qb), 1)

    def run(dil):
        sub_len = t // dil
        nb = sub_len // qb

        def body(it, carry):
            todo = range(ATTN_BLOCKS_PER_ITER)
            q_rows, qs, kws, vws, valid = [], [], [], [], []
            for u in todo:
                blk = it * ATTN_BLOCKS_PER_ITER + u
                rho = blk // nb
                bq = blk % nb
                rows = pl.ds(rho + dil * qb * bq, qb, stride=dil)
                s0 = jnp.maximum(qb * bq - HALF, 0)
                s3 = jnp.minimum(qb * bq + qb, sub_len - HALF)
                rows0 = pl.ds(rho + dil * s0, HALF, stride=dil)
                rows3 = pl.ds(rho + dil * s3, HALF, stride=dil)

                def window(ref):
                    return jnp.concatenate([ref[0, 0, rows0, :], ref[0, 0, rows, :], ref[0, 0, rows3, :]], axis=0)

                q_rows.append(rows)
                qs.append(q_ref[0, 0, rows, :].astype(_MXU_DTYPE))
                kws.append(window(k_ref).astype(_MXU_DTYPE))
                vws.append(window(v_ref).astype(_MXU_DTYPE))
                pos = qb * bq - HALF + col
                valid.append((pos >= 0) & (pos < sub_len))
            bias_b = bias_scr[...]
            s = [jnp.where(valid[u], _dot_nt(qs[u], kws[u]) * scale + bias_b, NEG) for u in todo]
            m_blk = [jnp.max(s[u], axis=-1, keepdims=True) for u in todo]
            m_old = [m_scr[q_rows[u], :] for u in todo]
            l_old = [l_scr[q_rows[u], :] for u in todo]
            acc_old = [acc_scr[q_rows[u], :] for u in todo]
            m_new = [jnp.maximum(m_old[u], m_blk[u]) for u in todo]
            alpha = [jnp.exp(m_old[u] - m_new[u]) for u in todo]
            p = [jnp.exp(s[u] - m_new[u][:, 0:1]) for u in todo]
            l_new = [alpha[u] * l_old[u] + jnp.sum(p[u], axis=-1, keepdims=True) for u in todo]
            acc = [alpha[u] * acc_old[u] + _dot(p[u].astype(_MXU_DTYPE), vws[u]) for u in todo]
            for u in todo:
                m_scr[q_rows[u], :] = m_new[u]
                l_scr[q_rows[u], :] = l_new[u]
                acc_scr[q_rows[u], :] = acc[u]
            return carry

        lax.fori_loop(0, dil * nb // ATTN_BLOCKS_PER_ITER, body, 0)

    for gidx, (_, dil) in enumerate(B_GROUPS):
        pl.when(gi == gidx)(functools.partial(run, dil))

    @pl.when(gi == N_GROUPS - 1)
    def _():
        o_ref[0] = (acc_scr[...] / l_scr[...]).astype(o_ref.dtype)


def _attention(rel_bias, buckets, q, k, v):
    bn, _, t, _ = q.shape
    qkv_spec = pl.BlockSpec((1, 1, t, B_HEAD_DIM), lambda b, h, g: (b, g * B_HPG + h, 0, 0))
    return pl.pallas_call(
        functools.partial(_attn_kernel, t=t),
        grid=(bn, B_HPG, N_GROUPS),
        in_specs=[
            pl.BlockSpec(memory_space=pltpu.SMEM),
            pl.BlockSpec((1, Q_BLOCK, 2 * Q_BLOCK), lambda b, h, g: (g, 0, 0)),
            qkv_spec, qkv_spec, qkv_spec,
        ],
        out_specs=pl.BlockSpec((1, t, B_HEAD_DIM), lambda b, h, g: (b, 0, h)),
        out_shape=jax.ShapeDtypeStruct((bn, t, B_WIDTH), _ACT_DTYPE),
        scratch_shapes=[
            pltpu.VMEM((t, B_HEAD_DIM), F32),
            pltpu.VMEM((t, B_HEAD_DIM), F32),
            pltpu.VMEM((t, B_HEAD_DIM), F32),
            pltpu.VMEM((Q_BLOCK, 2 * Q_BLOCK), F32),
        ],
        compiler_params=pltpu.CompilerParams(
            dimension_semantics=("parallel", "parallel", "arbitrary"), vmem_limit_bytes=VMEM_LIMIT_BYTES),
        name="attention",
    )(rel_bias, buckets, q, k, v)


def _merge_kernel(x_ref, mod_ref, y_ref, z_ref, o_ref, pg_ref, lw_ref, lb_ref, ones_ref, wa_ref, wb_ref, wo_ref,
                  out_ref):
    y =y_ref[0, 0] + y_ref[0, 1]
    head_ones = ones_ref[...]
    mean = _head_sums(y, head_ones) * (1.0 / A_HEAD_DIM)
    yc = y - mean
    var = _head_sums(yc * yc, head_ones) * (1.0 / A_HEAD_DIM)
    yn = yc * lax.rsqrt(var + GN_EPS) * lw_ref[...] + lb_ref[...]
    ya = yn * z_ref[0, 0].astype(F32) + z_ref[0, 1].astype(F32)
    pg = pg_ref[0].astype(F32)
    yb = o_ref[0].astype(F32) * _silu(pg[:, 0:B_WIDTH])
    gate_a = _sigmoid(pg[:, B_WIDTH:B_WIDTH + D_MODEL])
    gate_b = _sigmoid(pg[:, B_WIDTH + D_MODEL:])
    merged = (gate_a * _dot(ya.astype(_MXU_DTYPE), wa_ref[...])
              + gate_b * _dot(yb.astype(_MXU_DTYPE), wb_ref[...]))
    out_ref[0] = x_ref[0] + mod_ref[0, 2:3, :] * _dot(merged.astype(_MXU_DTYPE), wo_ref[...])


def _merge(x, mod3, y, z, o, pg, lnx_w, lnx_b, wa_b, wb_b, wo_b):
    bn, t, _ = x.shape
    tm = MERGE_ROWS
    const2 = lambda b, i: (0, 0)
    return pl.pallas_call(
        _merge_kernel,
        grid=(bn, t // tm),
        in_specs=[
            pl.BlockSpec((1, tm, D_MODEL), lambda b, i: (b, i, 0)),
            pl.BlockSpec((1, 3, D_MODEL), lambda b, i: (b, 0, 0)),
            pl.BlockSpec((1, 2, tm, A_WIDTH), lambda b, i: (b, 0, i, 0)),
            pl.BlockSpec((1, 2, tm, A_WIDTH), lambda b, i: (b, 0, i, 0)),
            pl.BlockSpec((1, tm, B_WIDTH), lambda b, i: (b, i, 0)),
            pl.BlockSpec((1, tm, PG_COLS), lambda b, i: (b, i, 0)),
            pl.BlockSpec((1, A_WIDTH), const2),
            pl.BlockSpec((1, A_WIDTH), const2),
            pl.BlockSpec((A_WIDTH, A_WIDTH), const2),
            pl.BlockSpec((A_WIDTH, D_MODEL), const2),
            pl.BlockSpec((B_WIDTH, D_MODEL), const2),
            pl.BlockSpec((D_MODEL, D_MODEL), const2),
        ],
        out_specs=pl.BlockSpec((1, tm, D_MODEL), lambda b, i: (b, i, 0)),
        out_shape=jax.ShapeDtypeStruct((bn, t, D_MODEL), x.dtype),
        compiler_params=pltpu.CompilerParams(
            dimension_semantics=("parallel", "parallel"), vmem_limit_bytes=VMEM_LIMIT_BYTES),
        name="merge",
    )(x, mod3, y, z, o, pg, lnx_w.reshape(1, A_WIDTH), lnx_b.reshape(1, A_WIDTH), _head_ones(), wa_b, wb_b,
      wo_b)


def _pad_low_rank(w):
    z = jnp.zeros((LOW_RANK, A_WIDTH), w.dtype)
    return jnp.stack([jnp.concatenate([w[0], z], axis=0), jnp.concatenate([z, w[1]], axis=0)])


def kernel(x_prompt, x_sample, c_prompt, c_sample, rel_bias, norm_w, w_ada, b_ada, w_in, shift_mu, w0, w2, a0, a2,
           k_k, k_a, r_k, lnx_w, lnx_b, q_gain, k_gain, w_out_a, w_out_b, w_out):
    depth = norm_w.shape[0]
    buckets = jnp.asarray(_band_buckets())

    def trunk(x, c):
        bn, t, _ = x.shape
        assert t % (Q_BLOCK * B_GROUPS[-1][1]) == 0 and t % MERGE_ROWS == 0
        for l in range(depth):
            mod3 = _modulation(c, w_ada[l], b_ada[l]).reshape(bn, 3, D_MODEL)
            pa, q, k, v, pg = _projection(x, mod3, norm_w[l], w_in[l].astype(_MXU_DTYPE), q_gain[l], k_gain[l])
            y, z = _rwkv(pa, shift_mu[l], w0[l], _pad_low_rank(w2[l]), a0[l], _pad_low_rank(a2[l]),
                         k_k[l], k_a[l], r_k[l].reshape(A_WIDTH))
            o = _attention(rel_bias, buckets, q, k, v)
            x = _merge(x, mod3, y, z, o, pg, lnx_w[l], lnx_b[l], w_out_a[l].astype(_MXU_DTYPE),
                       w_out_b[l].astype(_MXU_DTYPE), w_out[l].astype(_MXU_DTYPE))
        return x

    return (trunk(x_prompt, c_prompt), trunk(x_sample, c_sample))
```

```python
import functools
import math

import numpy as np
import jax
import jax.numpy as jnp
from jax import lax
from jax.experimental import pallas as pl
from jax.experimental.pallas import tpu as pltpu

F32 = jnp.float32
BF16 = jnp.bfloat16
HIGHEST = lax.Precision.HIGHEST

D_MODEL = 1024
A_HEADS = 8
A_HEAD_DIM = 64
A_WIDTH = A_HEADS * A_HEAD_DIM
LOW_RANK = 64
A_COLS = 4 * A_WIDTH + 4 * LOW_RANK
GN_EPS = 64e-5
B_GROUPS = ((128, 1), (512, 4), (2048, 16))
N_GROUPS = len(B_GROUPS)
B_HPG = 4
B_HEAD_DIM = 128
B_HEADS = N_GROUPS * B_HPG
B_QKV = B_HEADS * B_HEAD_DIM
B_WIDTH = B_HPG * B_HEAD_DIM
Q_BLOCK = 128
HALF = 64
N_BUCKETS = 32
MAX_EXACT = 8
MAX_DISTANCE = 1024
PG_COLS = B_WIDTH + 2 * D_MODEL
IN_COLS = A_COLS + 3 * B_QKV + PG_COLS
RMS_EPS = 1e-6
NEG = -1e30

LANES = 128
SUBLANES = 8
BF16_SUBLANES = 16
MXU_DIM = 256
VMEM_LIMIT_BYTES = 60000 * 1024

PROJ_ROWS = 256
CHUNK = 64
PREP_ROWS = 256
SCAN_SUB = 4
HEADS_PER_STACK = MXU_DIM // CHUNK
STACK_LANES = HEADS_PER_STACK * A_HEAD_DIM
N_STACKS = A_HEADS // HEADS_PER_STACK
MERGE_ROWS = 512
ATTN_BLOCKS_PER_ITER = 4

_MXU_DTYPE = BF16
_ACT_DTYPE = BF16


def _dot(a, b, precision=None):
    return jnp.dot(a, b, preferred_element_type=F32, precision=precision)


def _dot_nt(a, b, precision=None):
    return lax.dot_general(a, b, (((1,), (1,)), ((), ())), preferred_element_type=F32, precision=precision)


def _dot_tn(a, b, precision=None):
    return lax.dot_general(a, b, (((0,), (0,)), ((), ())), preferred_element_type=F32, precision=precision)


def _sigmoid(z):
    return 1.0 / (1.0 + jnp.exp(-z))


def _silu(z):
    return z * _sigmoid(z)


def _modulation_kernel(c_ref, w_ref, b_ref, o_ref):
    o_ref[...] = _dot(_silu(c_ref[...]), w_ref[...], HIGHEST) + b_ref[...]


def _modulation(c, w_ada, b_ada):
    bn = c.shape[0]
    return pl.pallas_call(
        _modulation_kernel,
        out_shape=jax.ShapeDtypeStruct((bn, 3 * D_MODEL), F32),
        compiler_params=pltpu.CompilerParams(vmem_limit_bytes=VMEM_LIMIT_BYTES),
        name="modulation",
    )(c, w_ada, b_ada.reshape(1, 3 * D_MODEL))


def _proj_kernel(x_ref, mod_ref, nw_ref, w_ref, qg_ref, kg_ref, pa_ref, q_ref, k_ref, v_ref, pg_ref):
    x = x_ref[0]
    ms = jnp.mean(x * x, axis=-1, keepdims=True)
    h = x * lax.rsqrt(ms + RMS_EPS) * nw_ref[...] * (1.0 + mod_ref[0, 1:2, :]) + mod_ref[0, 0:1, :]
    hb = h.astype(_MXU_DTYPE)
    step = 2 * MXU_DIM

    def cols(c0, width):
        return _dot(hb, w_ref[:, c0:c0 + width])

    for c0 in range(0, A_COLS, step):
        width = min(step, A_COLS - c0)
        pa_ref[0, :, c0:c0 + width] = cols(c0, width)

    def head_norm(z, gain):
        return z * lax.rsqrt(jnp.mean(z * z, axis=-1, keepdims=True) + RMS_EPS) * gain

    heads_per_step = step // B_HEAD_DIM
    for c0 in range(0, B_QKV, step):
        qc = cols(A_COLS + c0, step)
        kc = cols(A_COLS + B_QKV + c0, step)
        vc = cols(A_COLS + 2 * B_QKV + c0, step)
        for j in range(heads_per_step):
            hd = c0 // B_HEAD_DIM + j
            sl = slice(j * B_HEAD_DIM, (j + 1) * B_HEAD_DIM)
            q_ref[0, hd] = head_norm(qc[:, sl], qg_ref[...])
            k_ref[0, hd] = head_norm(kc[:, sl], kg_ref[...])
            v_ref[0, hd] = vc[:, sl]

    pg0 = A_COLS + 3 * B_QKV
    for c0 in range(0, PG_COLS, step):
        pg_ref[0, :, c0:c0 + step] = cols(pg0 + c0, step).astype(_ACT_DTYPE)


def _projection(x, mod3, norm_w, w_in_b, q_gain, k_gain):
    bn, t, _ = x.shape
    tm = PROJ_ROWS
    grid = (bn, t // tm)
    const2 = lambda b, i: (0, 0)
    return pl.pallas_call(
        _proj_kernel,
        grid=grid,
        in_specs=[
            pl.BlockSpec((1, tm, D_MODEL), lambda b, i: (b, i, 0)),
            pl.BlockSpec((1, 3, D_MODEL), lambda b, i: (b, 0, 0)),
            pl.BlockSpec((1, D_MODEL), const2),
            pl.BlockSpec((D_MODEL, IN_COLS), const2, pipeline_mode=pl.Buffered(1)),
            pl.BlockSpec((1, B_HEAD_DIM), const2),
            pl.BlockSpec((1, B_HEAD_DIM), const2),
        ],
        out_specs=[
            pl.BlockSpec((1, tm, A_COLS), lambda b, i: (b, i, 0)),
            pl.BlockSpec((1, B_HEADS, tm, B_HEAD_DIM), lambda b, i: (b, 0, i, 0)),
            pl.BlockSpec((1, B_HEADS, tm, B_HEAD_DIM), lambda b, i: (b, 0, i, 0)),
            pl.BlockSpec((1, B_HEADS, tm, B_HEAD_DIM), lambda b, i: (b, 0, i, 0)),
            pl.BlockSpec((1, tm, PG_COLS), lambda b, i: (b, i, 0)),
        ],
        out_shape=[
            jax.ShapeDtypeStruct((bn, t, A_COLS), F32),
            jax.ShapeDtypeStruct((bn, B_HEADS, t, B_HEAD_DIM), F32),
            jax.ShapeDtypeStruct((bn, B_HEADS, t, B_HEAD_DIM), F32),
            jax.ShapeDtypeStruct((bn, B_HEADS, t, B_HEAD_DIM), F32),
            jax.ShapeDtypeStruct((bn, t, PG_COLS), _ACT_DTYPE),
        ],
        compiler_params=pltpu.CompilerParams(
            dimension_semantics=("parallel", "parallel"), vmem_limit_bytes=VMEM_LIMIT_BYTES),
        name="projection",
    )(x, mod3, norm_w.reshape(1, D_MODEL), w_in_b, q_gain.reshape(1, B_HEAD_DIM), k_gain.reshape(1, B_HEAD_DIM))


def _split3(z):
    hi = z.astype(BF16)
    r1 = z - hi.astype(F32)
    mid = r1.astype(BF16)
    lo = (r1 - mid.astype(F32)).astype(BF16)
    return hi, mid, lo


def _dot_exact_lhs(lhs_b, z):
    hi, mid, lo = _split3(z)
    return _dot(lhs_b, hi) + (_dot(lhs_b, mid) + _dot(lhs_b, lo))


def _head_sums(z, head_ones_b):
    hi = z.astype(BF16)
    lo = (z - hi.astype(F32)).astype(BF16)
    return _dot(hi, head_ones_b) + _dot(lo, head_ones_b)


def _head_ones():
    head_of = np.arange(A_WIDTH) // A_HEAD_DIM
    return jnp.asarray(head_of[:, None] == head_of[None, :], dtype=BF16)


_MASK_STRICT, _MASK_INCL, _MASK_EYE, _MASK_LEVEL0 = 0, 1, 2, 3
_N_LEVELS = int(math.log2(CHUNK))
_N_MASKS = _MASK_LEVEL0 + _N_LEVELS
_OP_A, _OP_R, _OP_B, _OP_K, _OP_BTAIL, _OP_KTAIL = range(6)
_N_OPS = 6


def _scan_masks():
    idx = np.arange(MXU_DIM)
    out = []
    for d in range(2):
        p = idx % CHUNK if d == 0 else CHUNK - 1 - idx % CHUNK
        pi, pj = p[:, None], p[None, :]
        masks = [pj < pi, pj <= pi, idx[:, None] == idx[None, :]]
        for lv in range(_N_LEVELS):
            sz = 1 << lv
            masks.append(((pi // sz) % 2 == 1) & (pj // sz == pi // sz - 1))
        out.append(np.stack(masks))
    return np.stack(out).astype(np.float32)


def _cumulative_lhs():
    idx = np.arange(PREP_ROWS)
    same = idx[:, None] // CHUNK == idx[None, :] // CHUNK
    fwd = same & (idx[None, :] <= idx[:, None])
    bwd = same & (idx[None, :] >= idx[:, None])
    return np.stack([np.concatenate([fwd, same]), np.concatenate([bwd, same])]).astype(np.float32)


def _prep_kernel(pa_ref, prev_ref, next_ref, mu_ref, w0_ref, w2_ref, a0_ref, a2_ref, kk_ref, ka_ref, rk_ref, ones_ref,
                 cum_ref, ops_ref, v_ref, pend_ref, z_ref):
    i = pl.program_id(1)
    n = pl.num_programs(1)
    c = CHUNK
    rb = PREP_ROWS

    pa = pa_ref[0]
    prev_row = jnp.where(i > 0, prev_ref[0, SUBLANES - 1:SUBLANES, :], 0.0)
    next_row = jnp.where(i < n - 1, next_ref[0, 0:1, :], 0.0)
    row = lax.broadcasted_iota(jnp.int32, (rb, 1), 0)
    prev = jnp.where(row == 0, prev_row, pltpu.roll(pa, 1, 0))
    nxt = jnp.where(row == rb - 1, next_row, pltpu.roll(pa, rb - 1, 0))
    xs = pa + mu_ref[0:1, :] * (prev - pa) + mu_ref[1:2, :] * (nxt - pa)

    aw = A_WIDTH
    r = xs[:, 0:aw]
    k = xs[:, aw:2 * aw]
    v = xs[:, 2 * aw:3 * aw]
    g = xs[:, 3 * aw:4 * aw]
    tw = jnp.tanh(xs[:, 4 * aw:4 * aw + 2 * LOW_RANK]).astype(BF16)
    alo = xs[:, 4 * aw + 2 * LOW_RANK:4 * aw + 4 * LOW_RANK].astype(BF16)

    head_ones = ones_ref[...]
    kk = k * kk_ref[...]
    kk = kk * lax.rsqrt(jnp.maximum(_head_sums(kk * kk, head_ones), 1e-24))
    ka = ka_ref[...]
    a_dir = [_sigmoid(a0_ref[d] + _dot(alo, a2_ref[d])) for d in range(2)]
    k_both = k * (2.0 + (a_dir[0] + a_dir[1] - 2.0) * ka)
    bonus = _head_sums(r * k_both * rk_ref[...], head_ones) * v
    sg = _silu(g)
    z_ref[0, 0] = sg.astype(_ACT_DTYPE)
    z_ref[0, 1] = (bonus * sg).astype(_ACT_DTYPE)
    v_ref[0] = v.astype(BF16)

    for d in range(2):
        u = w0_ref[d] + _dot(tw, w2_ref[d])
        logw = -math.exp(-0.5) * _sigmoid(u)
        sums = _dot_exact_lhs(cum_ref[d], logw)
        cl = sums[0:rb]
        total = sums[rb:2 * rb]
        e_out = jnp.exp(-cl)
        e_tail = jnp.exp(total - cl)
        k_d = k * (1.0 + (a_dir[d] - 1.0) * ka)
        b_d = kk * a_dir[d]
        ops_ref[0, d, _OP_A] = (-kk * jnp.exp(cl - logw)).astype(BF16)
        ops_ref[0, d, _OP_R] = (r * jnp.exp(cl)).astype(BF16)
        ops_ref[0, d, _OP_B] = (b_d * e_out).astype(BF16)
        ops_ref[0, d, _OP_K] = (k_d * e_out).astype(BF16)
        ops_ref[0, d, _OP_BTAIL] = (b_d * e_tail).astype(BF16)
        ops_ref[0, d, _OP_KTAIL] = (k_d * e_tail).astype(BF16)
        p_end = jnp.exp(total)
        for j in range(rb // c):
            pend_ref[0, d, j] = p_end[j * c:j * c + 1, :]


def _prep(pa, shift_mu, w0, w2p, a0, a2p, k_k, k_a, r_k):
    bn, t, _ = pa.shape
    rb = PREP_ROWS
    n = t // rb
    rows8 = rb // SUBLANES
    head_ones = _head_ones()
    cum_lhs =jnp.asarray(_cumulative_lhs(), dtype=BF16)
    const2 = lambda b, i: (0, 0)
    const3 = lambda b, i: (0, 0, 0)
    return pl.pallas_call(
        _prep_kernel,
        grid=(bn, n),
        in_specs=[
            pl.BlockSpec((1, rb, A_COLS), lambda b, i: (b, i, 0)),
            pl.BlockSpec((1, SUBLANES, A_COLS), lambda b, i: (b, jnp.maximum(i * rows8 - 1, 0), 0)),
            pl.BlockSpec((1, SUBLANES, A_COLS), lambda b, i: (b, jnp.minimum((i + 1) * rows8, t // SUBLANES - 1), 0)),
            pl.BlockSpec((2, A_COLS), const2),
            pl.BlockSpec((2, 1, A_WIDTH), const3),
            pl.BlockSpec((2, 2 * LOW_RANK, A_WIDTH), const3),
            pl.BlockSpec((2, 1, A_WIDTH), const3),
            pl.BlockSpec((2, 2 * LOW_RANK, A_WIDTH), const3),
            pl.BlockSpec((1, A_WIDTH), const2),
            pl.BlockSpec((1, A_WIDTH), const2),
            pl.BlockSpec((1, A_WIDTH), const2),
            pl.BlockSpec((A_WIDTH, A_WIDTH), const2),
            pl.BlockSpec((2, 2 * rb, rb), const3),
        ],
        out_specs=[
            pl.BlockSpec((1, 2, _N_OPS, rb, A_WIDTH), lambda b, i: (b, 0, 0, i, 0)),
            pl.BlockSpec((1, rb, A_WIDTH), lambda b, i: (b, i, 0)),
            pl.BlockSpec((1, 2, rb // CHUNK, 1, A_WIDTH), lambda b, i: (b, 0, i, 0, 0)),
            pl.BlockSpec((1, 2, rb, A_WIDTH), lambda b, i: (b, 0, i, 0)),
        ],
        out_shape=[
            jax.ShapeDtypeStruct((bn, 2, _N_OPS, t, A_WIDTH), BF16),
            jax.ShapeDtypeStruct((bn, t, A_WIDTH), BF16),
            jax.ShapeDtypeStruct((bn, 2, t // CHUNK, 1, A_WIDTH), F32),
            jax.ShapeDtypeStruct((bn, 2, t, A_WIDTH), _ACT_DTYPE),
        ],
        compiler_params=pltpu.CompilerParams(
            dimension_semantics=("parallel", "parallel"), vmem_limit_bytes=VMEM_LIMIT_BYTES),
        name="rwkv_prep",
    )(pa, pa, pa, shift_mu, w0.reshape(2, 1, A_WIDTH), w2p.astype(BF16), a0.reshape(2, 1, A_WIDTH),
      a2p.astype(BF16), k_k.reshape(1, A_WIDTH), k_a.reshape(1, A_WIDTH), r_k.reshape(1, A_WIDTH), head_ones, cum_lhs)


def _scan_kernel(ops_ref, v_ref, pend_ref, mask_ref, y_ref, h_scr, rhat_scr, y0_scr, g_scr, hadd_scr):
    @pl.when(pl.program_id(2) == 0)
    def _():
        h_scr[...] = jnp.zeros_like(h_scr)
        rhat_scr[...] = jnp.zeros_like(rhat_scr)
        y0_scr[...] = jnp.zeros_like(y0_scr)
        g_scr[...] = jnp.zeros_like(g_scr)
        hadd_scr[...] = jnp.zeros_like(hadd_scr)

    for dd in range(2):
        pl.when(pl.program_id(1) == dd)(functools.partial(
            _scan_direction, dd, ops_ref, v_ref, pend_ref, mask_ref, y_ref, h_scr, rhat_scr, y0_scr, g_scr, hadd_scr))


def _scan_direction(dd, ops_ref, v_ref, pend_ref, mask_ref, y_ref, h_scr, rhat_scr, y0_scr, g_scr, hadd_scr):
    i = pl.program_id(2)
    c = CHUNK
    slot_new = i % 2
    slot_old = 1 - slot_new

    def odd_blocks(sz):
        per_chunk = c // sz
        out = []
        for blk in range(MXU_DIM // sz):
            order = blk % per_chunk if dd == 0 else per_chunk - 1 - blk % per_chunk
            out.append((blk * sz, order % 2 == 1))
        return out

    def take_odd(z, sz):
        return jnp.concatenate([z[s0:s0 + sz] for s0, odd in odd_blocks(sz) if odd], axis=0)

    def put_odd(z_odd, sz, base=None):
        parts, taken = [], 0
        for s0, odd in odd_blocks(sz):
            if odd:
                piece = z_odd[taken * sz:(taken + 1) * sz]
                taken += 1
                parts.append(piece if base is None else base[s0:s0 + sz] + piece)
            else:
                parts.append(jnp.zeros((sz, z_odd.shape[1]), z_odd.dtype) if base is None else base[s0:s0 + sz])
        return jnp.concatenate(parts, axis=0)

    probs = []
    for j in range(SCAN_SUB):
        jj = j if dd == 0 else SCAN_SUB - 1 - j
        rows = slice(jj * c, (jj + 1) * c)
        for s in range(N_STACKS):
            probs.append((jj, rows, s, slice(s * STACK_LANES, (s + 1) * STACK_LANES)))
    todo = range(len(probs))

    def carry(j):
        for p in range(j * N_STACKS, (j + 1) * N_STACKS):
            _, rows, s, sl = probs[p]
            h0 = h_scr[s].astype(BF16)
            y_s = _dot(rhat_scr[slot_old, p], h0) + y0_scr[slot_old, p]
            h_scr[s] = _dot(g_scr[slot_old, p], h0) + hadd_scr[slot_old, p]
            y = y_s[0:c]
            for hh in range(1, HEADS_PER_STACK):
                y = y + y_s[hh * c:(hh + 1) * c]
            y_ref[0, 0, rows, sl] = y

    n_stages = _N_LEVELS + 1
    carry_after = {(j * n_stages) // SCAN_SUB: j for j in range(SCAN_SUB)}

    def stage_done(stage):
        if stage in carry_after:
            carry(carry_after[stage])

    stage_done(0)
    lane_head = lax.broadcasted_iota(jnp.int32, (1, STACK_LANES), 1) // A_HEAD_DIM
    head_sel = [lane_head == hh for hh in range(HEADS_PER_STACK)]
    strict = mask_ref[0, _MASK_STRICT] != 0.0
    incl = mask_ref[0, _MASK_INCL] != 0.0
    eye = mask_ref[0, _MASK_EYE]

    def stack(ref_slice):
        return jnp.concatenate([jnp.where(head_sel[hh], ref_slice, jnp.zeros_like(ref_slice))
                                for hh in range(HEADS_PER_STACK)], axis=0)

    st = []
    for jj, rows, s, sl in probs:
        st.append({name: stack(ops_ref[0, 0, op, rows, sl]) for name, op in
                   (("a", _OP_A), ("r", _OP_R), ("b", _OP_B), ("k", _OP_K), ("bt", _OP_BTAIL), ("kt", _OP_KTAIL))})
        st[-1]["v"] = stack(v_ref[0, rows, sl])
    l_ab = [jnp.where(strict, _dot_nt(q["a"], q["b"]), 0.0) for q in st]
    l_ak = [jnp.where(strict, _dot_nt(q["a"], q["k"]), 0.0).astype(BF16) for q in st]
    m_rb = [jnp.where(incl, _dot_nt(q["r"], q["b"]), 0.0).astype(BF16) for q in st]
    m_rk = [jnp.where(incl, _dot_nt(q["r"], q["k"]), 0.0).astype(BF16) for q in st]
    stage_done(1)
    inv = [eye + l * mask_ref[0, _MASK_LEVEL0] for l in l_ab]
    for lv in range(1, _N_LEVELS):
        sz = 1 << lv
        inv_b = [x.astype(BF16) for x in inv]
        if sz >= BF16_SUBLANES:
            lvl = take_odd(mask_ref[0, _MASK_LEVEL0 + lv], sz)
            lo = [(take_odd(l, sz) * lvl).astype(BF16) for l in l_ab]
            tmp = [put_odd(_dot(lo[p], inv_b[p]).astype(BF16), sz) for p in todo]
            inv = [put_odd(_dot(take_odd(inv_b[p], sz), tmp[p]), sz, inv[p]) for p in todo]
        else:
            lo = [(l * mask_ref[0, _MASK_LEVEL0 + lv]).astype(BF16) for l in l_ab]
            tmp = [_dot(lo[p], inv_b[p]).astype(BF16) for p in todo]
            inv = [inv[p] + _dot(inv_b[p], tmp[p]) for p in todo]
        stage_done(lv + 1)
    inv_b = [x.astype(BF16) for x in inv]
    a_hat = [_dot(inv_b[p], st[p]["a"]).astype(BF16) for p in todo]
    w = [_dot(l_ak[p], st[p]["v"]).astype(BF16) for p in todo]
    u0 = [_dot(inv_b[p], w[p]).astype(BF16) for p in todo]
    for p in todo:
        rhat_scr[slot_new, p] = (st[p]["r"].astype(F32) + _dot(m_rb[p], a_hat[p])).astype(BF16)
        y0_scr[slot_new, p] = _dot(m_rb[p], u0[p]) + _dot(m_rk[p], st[p]["v"])
        g_scr[slot_new, p] = (eye * pend_ref[0, 0, probs[p][0]][:, probs[p][3]]
                              + _dot_tn(st[p]["bt"], a_hat[p])).astype(BF16)
        hadd_scr[slot_new, p] = _dot_tn(st[p]["bt"], u0[p]) + _dot_tn(st[p]["kt"], st[p]["v"])


def _scan(ops, v, pend):
    bn, _, _, t, _ = ops.shape
    rb = SCAN_SUB * CHUNK
    n = t // rb
    n_probs = SCAN_SUB * N_STACKS
    masks = jnp.asarray(_scan_masks())

    def block_of(d, k):
        return jnp.where(d == 0, k, n - 1 - k)

    def in_block(d, i):
        return block_of(d, jnp.minimum(i, n - 1))

    def out_block(d, i):
        return block_of(d, jnp.maximum(i - 1, 0))

    mat = (2, n_probs, MXU_DIM, MXU_DIM)
    return pl.pallas_call(
        _scan_kernel,
        grid=(bn, 2, n + 1),
        in_specs=[
            pl.BlockSpec((1, 1, _N_OPS, rb, A_WIDTH), lambda b, d, i: (b, d, 0, in_block(d, i), 0)),
            pl.BlockSpec((1, rb, A_WIDTH), lambda b, d, i: (b, in_block(d, i), 0)),
            pl.BlockSpec((1, 1, SCAN_SUB, 1, A_WIDTH), lambda b, d, i: (b, d, in_block(d, i), 0, 0)),
            pl.BlockSpec((1, _N_MASKS, MXU_DIM, MXU_DIM), lambda b, d, i: (d, 0, 0, 0)),
        ],
        out_specs=pl.BlockSpec((1, 1, rb, A_WIDTH), lambda b, d, i: (b, d, out_block(d, i), 0)),
        out_shape=jax.ShapeDtypeStruct((bn, 2, t, A_WIDTH), F32),
        scratch_shapes=[
            pltpu.VMEM((N_STACKS, MXU_DIM, MXU_DIM), F32),
            pltpu.VMEM(mat, BF16),
            pltpu.VMEM(mat, F32),
            pltpu.VMEM(mat, BF16),
            pltpu.VMEM(mat, F32),
        ],
        compiler_params=pltpu.CompilerParams(
            dimension_semantics=("parallel", "arbitrary", "arbitrary"), vmem_limit_bytes=VMEM_LIMIT_BYTES),
        name="rwkv_scan",
    )(ops, v, pend, masks)


def _rwkv(pa, shift_mu, w0, w2p, a0, a2p, k_k, k_a, r_k):
    ops, v, pend, z = _prep(pa, shift_mu, w0, w2p, a0, a2p, k_k, k_a, r_k)
    return _scan(ops, v, pend), z


def _t5_bucket_np(rel):
    half_b = N_BUCKETS // 2
    ret = np.where(rel > 0, half_b, 0)
    n = np.abs(rel)
    large = MAX_EXACT + (np.log(np.maximum(n, 1).astype(np.float32) / MAX_EXACT)
                         / math.log(MAX_DISTANCE / MAX_EXACT) * (half_b - MAX_EXACT)).astype(np.int32)
    large = np.minimum(large, half_b - 1)
    return ret + np.where(n < MAX_EXACT, n, large)


def _band_buckets():
    i = np.arange(Q_BLOCK)[:, None]
    j = np.arange(2 * Q_BLOCK)[None, :]
    rel = j - HALF - i
    out = []
    for _, dil in B_GROUPS:
        bkt = _t5_bucket_np(rel * dil)
        out.append(np.where(np.abs(rel) <= HALF, bkt, -1))
    return np.stack(out).astype(np.int32)


def _attn_kernel(tbl_ref, bkt_ref, q_ref, k_ref, v_ref, o_ref, m_scr, l_scr, acc_scr, bias_scr, *, t):
    h4 = pl.program_id(1)
    gi = pl.program_id(2)
    head = (N_GROUPS - 1 - gi) * B_HPG + h4
    qb = Q_BLOCK
    scale = B_HEAD_DIM ** -0.5

    bkt = bkt_ref[0]
    bias = jnp.full((qb, 2 * qb), NEG, F32)
    for bb in range(N_BUCKETS):
        bias = jnp.where(bkt == bb, tbl_ref[bb, head], bias)
    bias_scr[...] = bias
    col = lax.broadcasted_iota(jnp.int32, (1, 2 * qb), 1)

    def run(dil, first, last):
        sub_len = t // dil
        nb = sub_len // qb

        def body(it, carry):
            todo = range(ATTN_BLOCKS_PER_ITER)
            q_rows, qs, kws, vws, valid = [], [], [], [], []
            for u in todo:
                blk = it * ATTN_BLOCKS_PER_ITER + u
                rho = blk // nb
                bq = blk % nb
                s0 = jnp.maximum(qb * bq - HALF, 0)
                s3 = jnp.minimum(qb * bq + qb, sub_len - HALF)
                if dil == 1:
                    rows = pl.ds(pl.multiple_of(qb * bq, qb), qb)
                    rows0 = pl.ds(pl.multiple_of(s0, HALF), HALF)
                    rows3 = pl.ds(pl.multiple_of(s3, HALF), HALF)
                else:
                    rows = pl.ds(rho + dil * qb * bq, qb, stride=dil)
                    rows0 = pl.ds(rho + dil * s0, HALF, stride=dil)
                    rows3 = pl.ds(rho + dil * s3, HALF, stride=dil)

                def window(ref):
                    return jnp.concatenate([ref[0, 0, rows0, :], ref[0, 0, rows, :], ref[0, 0, rows3, :]], axis=0)

                q_rows.append(rows)
                qs.append(q_ref[0, 0, rows, :].astype(_MXU_DTYPE))
                kws.append(window(k_ref).astype(_MXU_DTYPE))
                vws.append(window(v_ref).astype(_MXU_DTYPE))
                pos = qb * bq - HALF + col
                valid.append((pos >= 0) & (pos < sub_len))
            bias_b = bias_scr[...]
            s = [jnp.where(valid[u], _dot_nt(qs[u], kws[u]) * scale + bias_b, NEG) for u in todo]
            m_blk = [jnp.max(s[u], axis=-1, keepdims=True) for u in todo]
            wide = (qb, B_HEAD_DIM)
            if first:
                m_new = [jnp.broadcast_to(m_blk[u], wide) for u in todo]
                p = [jnp.exp(s[u] - m_blk[u]) for u in todo]
                l_new = [jnp.broadcast_to(jnp.sum(p[u], axis=-1, keepdims=True), wide) for u in todo]
                acc = [_dot(p[u].astype(_MXU_DTYPE), vws[u]) for u in todo]
            else:
                m_old = [m_scr[q_rows[u], :] for u in todo]
                l_old = [l_scr[q_rows[u], :] for u in todo]
                acc_old = [acc_scr[q_rows[u], :] for u in todo]
                m_new = [jnp.maximum(m_old[u], m_blk[u]) for u in todo]
                alpha = [jnp.exp(m_old[u] - m_new[u]) for u in todo]
                p = [jnp.exp(s[u] - m_new[u][:, 0:1]) for u in todo]
                l_new = [alpha[u] * l_old[u] + jnp.sum(p[u], axis=-1, keepdims=True) for u in todo]
                acc = [alpha[u] * acc_old[u] + _dot(p[u].astype(_MXU_DTYPE), vws[u]) for u in todo]
            for u in todo:
                if last:
                    o_ref[0, q_rows[u], :] = (acc[u] / l_new[u]).astype(o_ref.dtype)
                else:
                    m_scr[q_rows[u], :] = m_new[u]
                    l_scr[q_rows[u], :] = l_new[u]
                    acc_scr[q_rows[u], :] = acc[u]
            return carry

        lax.fori_loop(0, dil * nb // ATTN_BLOCKS_PER_ITER, body, 0)

    assert B_GROUPS[0][1] == 1
    for gidx, (_, dil) in enumerate(B_GROUPS):
        pl.when(gi == N_GROUPS - 1 - gidx)(functools.partial(run, dil, gidx == N_GROUPS - 1, gidx == 0))


def _attention(rel_bias, buckets, q, k, v):
    bn, _, t, _ = q.shape
    qkv_spec = pl.BlockSpec((1, 1, t, B_HEAD_DIM), lambda b, h, g: (b, (N_GROUPS - 1 - g) * B_HPG + h, 0, 0))
    return pl.pallas_call(
        functools.partial(_attn_kernel, t=t),
        grid=(bn, B_HPG, N_GROUPS),
        in_specs=[
            pl.BlockSpec(memory_space=pltpu.SMEM),
            pl.BlockSpec((1, Q_BLOCK, 2 * Q_BLOCK), lambda b, h, g: (N_GROUPS - 1 - g, 0, 0)),
            qkv_spec, qkv_spec, qkv_spec,
        ],
        out_specs=pl.BlockSpec((1, t, B_HEAD_DIM), lambda b, h, g: (b, 0, h)),
        out_shape=jax.ShapeDtypeStruct((bn, t, B_WIDTH), _ACT_DTYPE),
        scratch_shapes=[
            pltpu.VMEM((t, B_HEAD_DIM), F32),
            pltpu.VMEM((t, B_HEAD_DIM), F32),
            pltpu.VMEM((t, B_HEAD_DIM), F32),
            pltpu.VMEM((Q_BLOCK, 2 * Q_BLOCK), F32),
        ],
        compiler_params=pltpu.CompilerParams(
            dimension_semantics=("parallel", "parallel", "arbitrary"), vmem_limit_bytes=VMEM_LIMIT_BYTES),
        name="attention",
    )(rel_bias, buckets, q, k, v)


def _merge_kernel(x_ref, mod_ref, y_ref, z_ref, o_ref, pg_ref, lw_ref, lb_ref, ones_ref, wa_ref, wb_ref, wo_ref,
                  out_ref):
    y =y_ref[0, 0] + y_ref[0, 1]
    head_ones = ones_ref[...]
    mean = _head_sums(y, head_ones) * (1.0 / A_HEAD_DIM)
    yc = y - mean
    var = _head_sums(yc * yc, head_ones) * (1.0 / A_HEAD_DIM)
    yn = yc * lax.rsqrt(var + GN_EPS) * lw_ref[...] + lb_ref[...]
    ya = yn * z_ref[0, 0].astype(F32) + z_ref[0, 1].astype(F32)
    pg = pg_ref[0].astype(F32)
    yb = o_ref[0].astype(F32) * _silu(pg[:, 0:B_WIDTH])
    gate_a = _sigmoid(pg[:, B_WIDTH:B_WIDTH + D_MODEL])
    gate_b = _sigmoid(pg[:, B_WIDTH + D_MODEL:])
    merged = (gate_a * _dot(ya.astype(_MXU_DTYPE), wa_ref[...])
              + gate_b * _dot(yb.astype(_MXU_DTYPE), wb_ref[...]))
    out_ref[0] = x_ref[0] + mod_ref[0, 2:3, :] * _dot(merged.astype(_MXU_DTYPE), wo_ref[...])


def _merge(x, mod3, y, z, o, pg, lnx_w, lnx_b, wa_b, wb_b, wo_b):
    bn, t, _ = x.shape
    tm = MERGE_ROWS
    const2 = lambda b, i: (0, 0)
    return pl.pallas_call(
        _merge_kernel,
        grid=(bn, t // tm),
        in_specs=[
            pl.BlockSpec((1, tm, D_MODEL), lambda b, i: (b, i, 0)),
            pl.BlockSpec((1, 3, D_MODEL), lambda b, i: (b, 0, 0)),
            pl.BlockSpec((1, 2, tm, A_WIDTH), lambda b, i: (b, 0, i, 0)),
            pl.BlockSpec((1, 2, tm, A_WIDTH), lambda b, i: (b, 0, i, 0)),
            pl.BlockSpec((1, tm, B_WIDTH), lambda b, i: (b, i, 0)),
            pl.BlockSpec((1, tm, PG_COLS), lambda b, i: (b, i, 0)),
            pl.BlockSpec((1, A_WIDTH), const2),
            pl.BlockSpec((1, A_WIDTH), const2),
            pl.BlockSpec((A_WIDTH, A_WIDTH), const2),
            pl.BlockSpec((A_WIDTH, D_MODEL), const2),
            pl.BlockSpec((B_WIDTH, D_MODEL), const2),
            pl.BlockSpec((D_MODEL, D_MODEL), const2),
        ],
        out_specs=pl.BlockSpec((1, tm, D_MODEL), lambda b, i: (b, i, 0)),
        out_shape=jax.ShapeDtypeStruct((bn, t, D_MODEL), x.dtype),
        compiler_params=pltpu.CompilerParams(
            dimension_semantics=("parallel", "parallel"), vmem_limit_bytes=VMEM_LIMIT_BYTES),
        name="merge",
    )(x, mod3, y, z, o, pg, lnx_w.reshape(1, A_WIDTH), lnx_b.reshape(1, A_WIDTH), _head_ones(), wa_b, wb_b,
      wo_b)


def _pad_low_rank(w):
    z = jnp.zeros((LOW_RANK, A_WIDTH), w.dtype)
    return jnp.stack([jnp.concatenate([w[0], z], axis=0), jnp.concatenate([z, w[1]], axis=0)])


def kernel(x_prompt, x_sample, c_prompt, c_sample, rel_bias, norm_w, w_ada, b_ada, w_in, shift_mu, w0, w2, a0, a2,
           k_k, k_a, r_k, lnx_w, lnx_b, q_gain, k_gain, w_out_a, w_out_b, w_out):
    depth = norm_w.shape[0]
    buckets = jnp.asarray(_band_buckets())

    def trunk(x, c):
        bn, t, _ = x.shape
        assert t % (Q_BLOCK * B_GROUPS[-1][1]) == 0 and t % MERGE_ROWS == 0
        for l in range(depth):
            mod3 = _modulation(c, w_ada[l], b_ada[l]).reshape(bn, 3, D_MODEL)
            pa, q, k, v, pg = _projection(x, mod3, norm_w[l], w_in[l].astype(_MXU_DTYPE), q_gain[l], k_gain[l])
            y, z = _rwkv(pa, shift_mu[l], w0[l], _pad_low_rank(w2[l]), a0[l], _pad_low_rank(a2[l]),
                         k_k[l], k_a[l], r_k[l].reshape(A_WIDTH))
            o = _attention(rel_bias, buckets, q, k, v)
            x = _merge(x, mod3, y, z, o, pg, lnx_w[l], lnx_b[l], w_out_a[l].astype(_MXU_DTYPE),
                       w_out_b[l].astype(_MXU_DTYPE), w_out[l].astype(_MXU_DTYPE))
        return x

    return (trunk(x_prompt, c_prompt), trunk(x_sample, c_sample))
```

```python
import functools
import math

import numpy as np
import jax
import jax.numpy as jnp
from jax import lax
from jax.experimental import pallas as pl
from jax.experimental.pallas import tpu as pltpu

F32 = jnp.float32
BF16 = jnp.bfloat16
HIGHEST = lax.Precision.HIGHEST

D_MODEL = 1024
A_HEADS = 8
A_HEAD_DIM = 64
A_WIDTH = A_HEADS * A_HEAD_DIM
LOW_RANK = 64
A_COLS = 4 * A_WIDTH + 4 * LOW_RANK
GN_EPS = 64e-5
B_GROUPS = ((128, 1), (512, 4), (2048, 16))
N_GROUPS = len(B_GROUPS)
B_HPG = 4
B_HEAD_DIM = 128
B_HEADS = N_GROUPS * B_HPG
B_QKV = B_HEADS * B_HEAD_DIM
B_WIDTH = B_HPG * B_HEAD_DIM
Q_BLOCK = 128
HALF = 64
N_BUCKETS = 32
MAX_EXACT = 8
MAX_DISTANCE = 1024
PG_COLS = B_WIDTH + 2 * D_MODEL
IN_COLS = A_COLS + 3 * B_QKV + PG_COLS
RMS_EPS = 1e-6
NEG = -1e30

LANES = 128
SUBLANES = 8
BF16_SUBLANES = 16
MXU_DIM = 256
VMEM_LIMIT_BYTES = 60000 * 1024

PROJ_ROWS = 256
CHUNK = 64
PREP_ROWS = 256
SCAN_SUB = 4
HEADS_PER_STACK = MXU_DIM // CHUNK
STACK_LANES = HEADS_PER_STACK * A_HEAD_DIM
N_STACKS = A_HEADS // HEADS_PER_STACK
MERGE_ROWS = 512
ATTN_BLOCKS_PER_ITER = 4

_MXU_DTYPE = BF16
_ACT_DTYPE = BF16


def _dot(a, b, precision=None):
    return jnp.dot(a, b, preferred_element_type=F32, precision=precision)


def _dot_nt(a, b, precision=None):
    return lax.dot_general(a, b, (((1,), (1,)), ((), ())), preferred_element_type=F32, precision=precision)


def _dot_tn(a, b, precision=None):
    return lax.dot_general(a, b, (((0,), (0,)), ((), ())), preferred_element_type=F32, precision=precision)


def _sigmoid(z):
    return 1.0 / (1.0 + jnp.exp(-z))


def _silu(z):
    return z * _sigmoid(z)


def _modulation_kernel(c_ref, w_ref, b_ref, o_ref):
    o_ref[...] = _dot(_silu(c_ref[...]), w_ref[...], HIGHEST) + b_ref[...]


def _modulation(c, w_ada, b_ada):
    bn = c.shape[0]
    return pl.pallas_call(
        _modulation_kernel,
        out_shape=jax.ShapeDtypeStruct((bn, 3 * D_MODEL), F32),
        compiler_params=pltpu.CompilerParams(vmem_limit_bytes=VMEM_LIMIT_BYTES),
        name="modulation",
    )(c, w_ada, b_ada.reshape(1, 3 * D_MODEL))


def _proj_kernel(x_ref, mod_ref, nw_ref, w_ref, qg_ref, kg_ref, pa_ref, q_ref, k_ref, v_ref, pg_ref):
    x = x_ref[0]
    ms = jnp.mean(x * x, axis=-1, keepdims=True)
    h = x * lax.rsqrt(ms + RMS_EPS) * nw_ref[...] * (1.0 + mod_ref[0, 1:2, :]) + mod_ref[0, 0:1, :]
    hb = h.astype(_MXU_DTYPE)
    step = 2 * MXU_DIM

    def cols(c0, width):
        return _dot(hb, w_ref[:, c0:c0 + width])

    for c0 in range(0, A_COLS, step):
        width = min(step, A_COLS - c0)
        pa_ref[0, :, c0:c0 + width] = cols(c0, width)

    def head_norm(z, gain):
        return z * lax.rsqrt(jnp.mean(z * z, axis=-1, keepdims=True) + RMS_EPS) * gain

    heads_per_step = step // B_HEAD_DIM
    for c0 in range(0, B_QKV, step):
        qc = cols(A_COLS + c0, step)
        kc = cols(A_COLS + B_QKV + c0, step)
        vc = cols(A_COLS + 2 * B_QKV + c0, step)
        for j in range(heads_per_step):
            hd = c0 // B_HEAD_DIM + j
            sl = slice(j * B_HEAD_DIM, (j + 1) * B_HEAD_DIM)
            q_ref[0, hd] = head_norm(qc[:, sl], qg_ref[...])
            k_ref[0, hd] = head_norm(kc[:, sl], kg_ref[...])
            v_ref[0, hd] = vc[:, sl]

    pg0 = A_COLS + 3 * B_QKV
    for c0 in range(0, PG_COLS, step):
        pg_ref[0, :, c0:c0 + step] = cols(pg0 + c0, step).astype(_ACT_DTYPE)


def _projection(x, mod3, norm_w, w_in_b, q_gain, k_gain):
    bn, t, _ = x.shape
    tm = PROJ_ROWS
    grid = (bn, t // tm)
    const2 = lambda b, i: (0, 0)
    return pl.pallas_call(
        _proj_kernel,
        grid=grid,
        in_specs=[
            pl.BlockSpec((1, tm, D_MODEL), lambda b, i: (b, i, 0)),
            pl.BlockSpec((1, 3, D_MODEL), lambda b, i: (b, 0, 0)),
            pl.BlockSpec((1, D_MODEL), const2),
            pl.BlockSpec((D_MODEL, IN_COLS), const2, pipeline_mode=pl.Buffered(1)),
            pl.BlockSpec((1, B_HEAD_DIM), const2),
            pl.BlockSpec((1, B_HEAD_DIM), const2),
        ],
        out_specs=[
            pl.BlockSpec((1, tm, A_COLS), lambda b, i: (b, i, 0)),
            pl.BlockSpec((1, B_HEADS, tm, B_HEAD_DIM), lambda b, i: (b, 0, i, 0)),
            pl.BlockSpec((1, B_HEADS, tm, B_HEAD_DIM), lambda b, i: (b, 0, i, 0)),
            pl.BlockSpec((1, B_HEADS, tm, B_HEAD_DIM), lambda b, i: (b, 0, i, 0)),
            pl.BlockSpec((1, tm, PG_COLS), lambda b, i: (b, i, 0)),
        ],
        out_shape=[
            jax.ShapeDtypeStruct((bn, t, A_COLS), F32),
            jax.ShapeDtypeStruct((bn, B_HEADS, t, B_HEAD_DIM), F32),
            jax.ShapeDtypeStruct((bn, B_HEADS, t, B_HEAD_DIM), F32),
            jax.ShapeDtypeStruct((bn, B_HEADS, t, B_HEAD_DIM), F32),
            jax.ShapeDtypeStruct((bn, t, PG_COLS), _ACT_DTYPE),
        ],
        compiler_params=pltpu.CompilerParams(
            dimension_semantics=("parallel", "parallel"), vmem_limit_bytes=VMEM_LIMIT_BYTES),
        name="projection",
    )(x, mod3, norm_w.reshape(1, D_MODEL), w_in_b, q_gain.reshape(1, B_HEAD_DIM), k_gain.reshape(1, B_HEAD_DIM))


def _split3(z):
    hi = z.astype(BF16)
    r1 = z - hi.astype(F32)
    mid = r1.astype(BF16)
    lo = (r1 - mid.astype(F32)).astype(BF16)
    return hi, mid, lo


def _dot_exact_lhs(lhs_b, z):
    hi, mid, lo = _split3(z)
    return _dot(lhs_b, hi) + (_dot(lhs_b, mid) + _dot(lhs_b, lo))


def _head_sums(z, head_ones_b):
    hi = z.astype(BF16)
    lo = (z - hi.astype(F32)).astype(BF16)
    return _dot(hi, head_ones_b) + _dot(lo, head_ones_b)


def _head_ones():
    head_of = np.arange(A_WIDTH) // A_HEAD_DIM
    return jnp.asarray(head_of[:, None] == head_of[None, :], dtype=BF16)


_MASK_STRICT, _MASK_INCL, _MASK_EYE, _MASK_LEVEL0 = 0, 1, 2, 3
_N_LEVELS = int(math.log2(CHUNK))
_N_MASKS = _MASK_LEVEL0 + _N_LEVELS
_OP_A, _OP_R, _OP_B, _OP_K, _OP_BTAIL, _OP_KTAIL = range(6)
_N_OPS = 6


def _scan_masks():
    idx = np.arange(MXU_DIM)
    out = []
    for d in range(2):
        p = idx % CHUNK if d == 0 else CHUNK - 1 - idx % CHUNK
        pi, pj = p[:, None], p[None, :]
        masks = [pj < pi, pj <= pi, idx[:, None] == idx[None, :]]
        for lv in range(_N_LEVELS):
            sz = 1 << lv
            masks.append(((pi // sz) % 2 == 1) & (pj // sz == pi // sz - 1))
        out.append(np.stack(masks))
    return np.stack(out).astype(np.float32)


def _cumulative_lhs():
    idx = np.arange(PREP_ROWS)
    same = idx[:, None] // CHUNK == idx[None, :] // CHUNK
    fwd = same & (idx[None, :] <= idx[:, None])
    bwd = same & (idx[None, :] >= idx[:, None])
    return np.stack([np.concatenate([fwd, same]), np.concatenate([bwd, same])]).astype(np.float32)


def _prep_kernel(pa_ref, prev_ref, next_ref, mu_ref, w0_ref, w2_ref, a0_ref, a2_ref, kk_ref, ka_ref, rk_ref, ones_ref,
                 cum_ref, ops_ref, v_ref, pend_ref, z_ref):
    i = pl.program_id(1)
    n = pl.num_programs(1)
    c = CHUNK
    rb = PREP_ROWS

    pa = pa_ref[0]
    prev_row = jnp.where(i > 0, prev_ref[0, SUBLANES - 1:SUBLANES, :], 0.0)
    next_row = jnp.where(i < n - 1, next_ref[0, 0:1, :], 0.0)
    row = lax.broadcasted_iota(jnp.int32, (rb, 1), 0)
    prev = jnp.where(row == 0, prev_row, pltpu.roll(pa, 1, 0))
    nxt = jnp.where(row == rb - 1, next_row, pltpu.roll(pa, rb - 1, 0))
    xs = pa + mu_ref[0:1, :] * (prev - pa) + mu_ref[1:2, :] * (nxt - pa)

    aw = A_WIDTH
    r = xs[:, 0:aw]
    k = xs[:, aw:2 * aw]
    v = xs[:, 2 * aw:3 * aw]
    g = xs[:, 3 * aw:4 * aw]
    tw = jnp.tanh(xs[:, 4 * aw:4 * aw + 2 * LOW_RANK]).astype(BF16)
    alo = xs[:, 4 * aw + 2 * LOW_RANK:4 * aw + 4 * LOW_RANK].astype(BF16)

    head_ones = ones_ref[...]
    kk = k * kk_ref[...]
    kk = kk * lax.rsqrt(jnp.maximum(_head_sums(kk * kk, head_ones), 1e-24))
    ka = ka_ref[...]
    a_dir = [_sigmoid(a0_ref[d] + _dot(alo, a2_ref[d])) for d in range(2)]
    k_both = k * (2.0 + (a_dir[0] + a_dir[1] - 2.0) * ka)
    bonus = _head_sums(r * k_both * rk_ref[...], head_ones) * v
    sg = _silu(g)
    z_ref[0, 0] = sg.astype(_ACT_DTYPE)
    z_ref[0, 1] = (bonus * sg).astype(_ACT_DTYPE)
    v_ref[0] = v.astype(BF16)

    for d in range(2):
        u = w0_ref[d] + _dot(tw, w2_ref[d])
        logw = -math.exp(-0.5) * _sigmoid(u)
        sums = _dot_exact_lhs(cum_ref[d], logw)
        cl = sums[0:rb]
        total = sums[rb:2 * rb]
        e_out = jnp.exp(-cl)
        e_tail = jnp.exp(total - cl)
        k_d = k * (1.0 + (a_dir[d] - 1.0) * ka)
        b_d = kk * a_dir[d]
        ops_ref[0, d, _OP_A] = (-kk * jnp.exp(cl - logw)).astype(BF16)
        ops_ref[0, d, _OP_R] = (r * jnp.exp(cl)).astype(BF16)
        ops_ref[0, d, _OP_B] = (b_d * e_out).astype(BF16)
        ops_ref[0, d, _OP_K] = (k_d * e_out).astype(BF16)
        ops_ref[0, d, _OP_BTAIL] = (b_d * e_tail).astype(BF16)
        ops_ref[0, d, _OP_KTAIL] = (k_d * e_tail).astype(BF16)
        p_end = jnp.exp(total)
        for j in range(rb // c):
            pend_ref[0, d, j] = p_end[j * c:j * c + 1, :]


def _prep(pa, shift_mu, w0, w2p, a0, a2p, k_k, k_a, r_k):
    bn, t, _ = pa.shape
    rb = PREP_ROWS
    n = t // rb
    rows8 = rb // SUBLANES
    head_ones = _head_ones()
    cum_lhs =jnp.asarray(_cumulative_lhs(), dtype=BF16)
    const2 = lambda b, i: (0, 0)
    const3 = lambda b, i: (0, 0, 0)
    return pl.pallas_call(
        _prep_kernel,
        grid=(bn, n),
        in_specs=[
            pl.BlockSpec((1, rb, A_COLS), lambda b, i: (b, i, 0)),
            pl.BlockSpec((1, SUBLANES, A_COLS), lambda b, i: (b, jnp.maximum(i * rows8 - 1, 0), 0)),
            pl.BlockSpec((1, SUBLANES, A_COLS), lambda b, i: (b, jnp.minimum((i + 1) * rows8, t // SUBLANES - 1), 0)),
            pl.BlockSpec((2, A_COLS), const2),
            pl.BlockSpec((2, 1, A_WIDTH), const3),
            pl.BlockSpec((2, 2 * LOW_RANK, A_WIDTH), const3),
            pl.BlockSpec((2, 1, A_WIDTH), const3),
            pl.BlockSpec((2, 2 * LOW_RANK, A_WIDTH), const3),
            pl.BlockSpec((1, A_WIDTH), const2),
            pl.BlockSpec((1, A_WIDTH), const2),
            pl.BlockSpec((1, A_WIDTH), const2),
            pl.BlockSpec((A_WIDTH, A_WIDTH), const2),
            pl.BlockSpec((2, 2 * rb, rb), const3),
        ],
        out_specs=[
            pl.BlockSpec((1, 2, _N_OPS, rb, A_WIDTH), lambda b, i: (b, 0, 0, i, 0)),
            pl.BlockSpec((1, rb, A_WIDTH), lambda b, i: (b, i, 0)),
            pl.BlockSpec((1, 2, rb // CHUNK, 1, A_WIDTH), lambda b, i: (b, 0, i, 0, 0)),
            pl.BlockSpec((1, 2, rb, A_WIDTH), lambda b, i: (b, 0, i, 0)),
        ],
        out_shape=[
            jax.ShapeDtypeStruct((bn, 2, _N_OPS, t, A_WIDTH), BF16),
            jax.ShapeDtypeStruct((bn, t, A_WIDTH), BF16),
            jax.ShapeDtypeStruct((bn, 2, t // CHUNK, 1, A_WIDTH), F32),
            jax.ShapeDtypeStruct((bn, 2, t, A_WIDTH), _ACT_DTYPE),
        ],
        compiler_params=pltpu.CompilerParams(
            dimension_semantics=("parallel", "parallel"), vmem_limit_bytes=VMEM_LIMIT_BYTES),
        name="rwkv_prep",
    )(pa, pa, pa, shift_mu, w0.reshape(2, 1, A_WIDTH), w2p.astype(BF16), a0.reshape(2, 1, A_WIDTH),
      a2p.astype(BF16), k_k.reshape(1, A_WIDTH), k_a.reshape(1, A_WIDTH), r_k.reshape(1, A_WIDTH), head_ones, cum_lhs)


def _scan_kernel(ops_ref, v_ref, pend_ref, mask_ref, y_ref, h_scr, rhat_scr, y0_scr, g_scr, hadd_scr):
    @pl.when(pl.program_id(2) == 0)
    def _():
        h_scr[...] = jnp.zeros_like(h_scr)
        rhat_scr[...] = jnp.zeros_like(rhat_scr)
        y0_scr[...] = jnp.zeros_like(y0_scr)
        g_scr[...] = jnp.zeros_like(g_scr)
        hadd_scr[...] = jnp.zeros_like(hadd_scr)

    last = pl.program_id(2) == pl.num_programs(2) - 1
    for dd in range(2):
        for prepare in (True, False):
            pl.when((pl.program_id(1) == dd) & (last != prepare))(functools.partial(
                _scan_direction, dd, prepare, ops_ref, v_ref, pend_ref, mask_ref, y_ref, h_scr, rhat_scr, y0_scr,
                g_scr, hadd_scr))


def _scan_direction(dd, prepare, ops_ref, v_ref, pend_ref, mask_ref, y_ref, h_scr, rhat_scr, y0_scr, g_scr, hadd_scr):
    i = pl.program_id(2)
    c = CHUNK
    slot_new = i % 2
    slot_old = 1 - slot_new

    def odd_blocks(sz):
        per_chunk = c // sz
        out = []
        for blk in range(MXU_DIM // sz):
            order = blk % per_chunk if dd == 0 else per_chunk - 1 - blk % per_chunk
            out.append((blk * sz, order % 2 == 1))
        return out

    def take_odd(z, sz):
        return jnp.concatenate([z[s0:s0 + sz] for s0, odd in odd_blocks(sz) if odd], axis=0)

    def put_odd(z_odd, sz, base=None):
        parts, taken = [], 0
        for s0, odd in odd_blocks(sz):
            if odd:
                piece = z_odd[taken * sz:(taken + 1) * sz]
                taken += 1
                parts.append(piece if base is None else base[s0:s0 + sz] + piece)
            else:
                parts.append(jnp.zeros((sz, z_odd.shape[1]), z_odd.dtype) if base is None else base[s0:s0 + sz])
        return jnp.concatenate(parts, axis=0)

    probs = []
    for j in range(SCAN_SUB):
        jj = j if dd == 0 else SCAN_SUB - 1 - j
        rows = slice(jj * c, (jj + 1) * c)
        for s in range(N_STACKS):
            probs.append((jj, rows, s, slice(s * STACK_LANES, (s + 1) * STACK_LANES)))
    todo = range(len(probs))

    def carry(j):
        for p in range(j * N_STACKS, (j + 1) * N_STACKS):
            _, rows, s, sl = probs[p]
            h0 = h_scr[s].astype(BF16)
            y_s = _dot(rhat_scr[slot_old, p], h0) + y0_scr[slot_old, p]
            h_scr[s] = _dot(g_scr[slot_old, p], h0) + hadd_scr[slot_old, p]
            y = y_s[0:c]
            for hh in range(1, HEADS_PER_STACK):
                y = y + y_s[hh * c:(hh + 1) * c]
            y_ref[0, 0, rows, sl] = y

    n_stages = _N_LEVELS + 1
    carry_after = {(j * n_stages) // SCAN_SUB: j for j in range(SCAN_SUB)}

    def stage_done(stage):
        if stage in carry_after:
            carry(carry_after[stage])

    if not prepare:
        for j in range(SCAN_SUB):
            carry(j)
        return

    stage_done(0)
    lane_head = lax.broadcasted_iota(jnp.int32, (1, STACK_LANES), 1) // A_HEAD_DIM
    head_sel = [lane_head == hh for hh in range(HEADS_PER_STACK)]
    strict = mask_ref[0, _MASK_STRICT] != 0.0
    incl = mask_ref[0, _MASK_INCL] != 0.0
    eye = mask_ref[0, _MASK_EYE]

    def stack(ref_slice):
        return jnp.concatenate([jnp.where(head_sel[hh], ref_slice, jnp.zeros_like(ref_slice))
                                for hh in range(HEADS_PER_STACK)], axis=0)

    st = []
    for jj, rows, s, sl in probs:
        st.append({name: stack(ops_ref[0, 0, op, rows, sl]) for name, op in
                   (("a", _OP_A), ("r", _OP_R), ("b", _OP_B), ("k", _OP_K), ("bt", _OP_BTAIL), ("kt", _OP_KTAIL))})
        st[-1]["v"] = stack(v_ref[0, rows, sl])
    l_ab = [jnp.where(strict, _dot_nt(q["a"], q["b"]), 0.0) for q in st]
    l_ak = [jnp.where(strict, _dot_nt(q["a"], q["k"]), 0.0).astype(BF16) for q in st]
    m_rb = [jnp.where(incl, _dot_nt(q["r"], q["b"]), 0.0).astype(BF16) for q in st]
    m_rk = [jnp.where(incl, _dot_nt(q["r"], q["k"]), 0.0).astype(BF16) for q in st]
    stage_done(1)
    inv = [eye + l * mask_ref[0, _MASK_LEVEL0] for l in l_ab]
    for lv in range(1, _N_LEVELS):
        sz = 1 << lv
        inv_b = [x.astype(BF16) for x in inv]
        if sz >= BF16_SUBLANES:
            lvl = take_odd(mask_ref[0, _MASK_LEVEL0 + lv], sz)
            lo = [(take_odd(l, sz) * lvl).astype(BF16) for l in l_ab]
            tmp = [put_odd(_dot(lo[p], inv_b[p]).astype(BF16), sz) for p in todo]
            inv = [put_odd(_dot(take_odd(inv_b[p], sz), tmp[p]), sz, inv[p]) for p in todo]
        else:
            lo = [(l * mask_ref[0, _MASK_LEVEL0 + lv]).astype(BF16) for l in l_ab]
            tmp = [_dot(lo[p], inv_b[p]).astype(BF16) for p in todo]
            inv = [inv[p] + _dot(inv_b[p], tmp[p]) for p in todo]
        stage_done(lv + 1)
    inv_b = [x.astype(BF16) for x in inv]
    a_hat = [_dot(inv_b[p], st[p]["a"]).astype(BF16) for p in todo]
    w = [_dot(l_ak[p], st[p]["v"]).astype(BF16) for p in todo]
    u0 = [_dot(inv_b[p], w[p]).astype(BF16) for p in todo]
    for p in todo:
        rhat_scr[slot_new, p] = (st[p]["r"].astype(F32) + _dot(m_rb[p], a_hat[p])).astype(BF16)
        y0_scr[slot_new, p] = _dot(m_rb[p], u0[p]) + _dot(m_rk[p], st[p]["v"])
        g_scr[slot_new, p] = (eye * pend_ref[0, 0, probs[p][0]][:, probs[p][3]]
                              + _dot_tn(st[p]["bt"], a_hat[p])).astype(BF16)
        hadd_scr[slot_new, p] = _dot_tn(st[p]["bt"], u0[p]) + _dot_tn(st[p]["kt"], st[p]["v"])


def _scan(ops, v, pend):
    bn, _, _, t, _ = ops.shape
    rb = SCAN_SUB * CHUNK
    n = t // rb
    n_probs = SCAN_SUB * N_STACKS
    masks = jnp.asarray(_scan_masks())

    def block_of(d, k):
        return jnp.where(d == 0, k, n - 1 - k)

    def in_block(d, i):
        return block_of(d, jnp.minimum(i, n - 1))

    def out_block(d, i):
        return block_of(d, jnp.maximum(i - 1, 0))

    mat = (2, n_probs, MXU_DIM, MXU_DIM)
    return pl.pallas_call(
        _scan_kernel,
        grid=(bn, 2, n + 1),
        in_specs=[
            pl.BlockSpec((1, 1, _N_OPS, rb, A_WIDTH), lambda b, d, i: (b, d, 0, in_block(d, i), 0)),
            pl.BlockSpec((1, rb, A_WIDTH), lambda b, d, i: (b, in_block(d, i), 0)),
            pl.BlockSpec((1, 1, SCAN_SUB, 1, A_WIDTH), lambda b, d, i: (b, d, in_block(d, i), 0, 0)),
            pl.BlockSpec((1, _N_MASKS, MXU_DIM, MXU_DIM), lambda b, d, i: (d, 0, 0, 0)),
        ],
        out_specs=pl.BlockSpec((1, 1, rb, A_WIDTH), lambda b, d, i: (b, d, out_block(d, i), 0)),
        out_shape=jax.ShapeDtypeStruct((bn, 2, t, A_WIDTH), F32),
        scratch_shapes=[
            pltpu.VMEM((N_STACKS, MXU_DIM, MXU_DIM), F32),
            pltpu.VMEM(mat, BF16),
            pltpu.VMEM(mat, F32),
            pltpu.VMEM(mat, BF16),
            pltpu.VMEM(mat, F32),
        ],
        compiler_params=pltpu.CompilerParams(
            dimension_semantics=("parallel", "arbitrary", "arbitrary"), vmem_limit_bytes=VMEM_LIMIT_BYTES),
        name="rwkv_scan",
    )(ops, v, pend, masks)


def _rwkv(pa, shift_mu, w0, w2p, a0, a2p, k_k, k_a, r_k):
    ops, v, pend, z = _prep(pa, shift_mu, w0, w2p, a0, a2p, k_k, k_a, r_k)
    return _scan(ops, v, pend), z


def _t5_bucket_np(rel):
    half_b = N_BUCKETS // 2
    ret = np.where(rel > 0, half_b, 0)
    n = np.abs(rel)
    large = MAX_EXACT + (np.log(np.maximum(n, 1).astype(np.float32) / MAX_EXACT)
                         / math.log(MAX_DISTANCE / MAX_EXACT) * (half_b - MAX_EXACT)).astype(np.int32)
    large = np.minimum(large, half_b - 1)
    return ret + np.where(n < MAX_EXACT, n, large)


def _band_buckets():
    i = np.arange(Q_BLOCK)[:, None]
    j = np.arange(2 * Q_BLOCK)[None, :]
    rel = j - HALF - i
    out = []
    for _, dil in B_GROUPS:
        bkt = _t5_bucket_np(rel * dil)
        out.append(np.where(np.abs(rel) <= HALF, bkt, -1))
    return np.stack(out).astype(np.int32)


def _attn_kernel(tbl_ref, bkt_ref, q_ref, k_ref, v_ref, o_ref, m_scr, l_scr, acc_scr, bias_scr, *, t):
    h4 = pl.program_id(1)
    gi = pl.program_id(2)
    head = (N_GROUPS - 1 - gi) * B_HPG + h4
    qb = Q_BLOCK
    scale = B_HEAD_DIM ** -0.5

    bkt = bkt_ref[0]
    bias = jnp.full((qb, 2 * qb), NEG, F32)
    for bb in range(N_BUCKETS):
        bias = jnp.where(bkt == bb, tbl_ref[bb, head], bias)
    bias_scr[...] = bias
    col = lax.broadcasted_iota(jnp.int32, (1, 2 * qb), 1)

    def run(dil, first, last):
        sub_len = t // dil
        nb = sub_len // qb

        def body(it, carry):
            todo = range(ATTN_BLOCKS_PER_ITER)
            q_rows, qs, kws, vws, valid = [], [], [], [], []
            for u in todo:
                blk = it * ATTN_BLOCKS_PER_ITER + u
                rho = blk // nb
                bq = blk % nb
                s0 = jnp.maximum(qb * bq - HALF, 0)
                s3 = jnp.minimum(qb * bq + qb, sub_len - HALF)
                if dil == 1:
                    rows = pl.ds(pl.multiple_of(qb * bq, qb), qb)
                    rows0 = pl.ds(pl.multiple_of(s0, HALF), HALF)
                    rows3 = pl.ds(pl.multiple_of(s3, HALF), HALF)
                else:
                    rows = pl.ds(rho + dil * qb * bq, qb, stride=dil)
                    rows0 = pl.ds(rho + dil * s0, HALF, stride=dil)
                    rows3 = pl.ds(rho + dil * s3, HALF, stride=dil)

                def window(ref):
                    return jnp.concatenate([ref[0, 0, rows0, :], ref[0, 0, rows, :], ref[0, 0, rows3, :]], axis=0)

                q_rows.append(rows)
                qs.append(q_ref[0, 0, rows, :].astype(_MXU_DTYPE))
                kws.append(window(k_ref).astype(_MXU_DTYPE))
                vws.append(window(v_ref).astype(_MXU_DTYPE))
                pos = qb * bq - HALF + col
                valid.append((pos >= 0) & (pos < sub_len))
            bias_b = bias_scr[...]
            s = [jnp.where(valid[u], _dot_nt(qs[u], kws[u]) * scale + bias_b, NEG) for u in todo]
            m_blk = [jnp.max(s[u], axis=-1, keepdims=True) for u in todo]
            wide = (qb, B_HEAD_DIM)
            if first:
                m_new = [jnp.broadcast_to(m_blk[u], wide) for u in todo]
                p = [jnp.exp(s[u] - m_blk[u]) for u in todo]
                l_new = [jnp.broadcast_to(jnp.sum(p[u], axis=-1, keepdims=True), wide) for u in todo]
                acc = [_dot(p[u].astype(_MXU_DTYPE), vws[u]) for u in todo]
            else:
                m_old = [m_scr[q_rows[u], :] for u in todo]
                l_old = [l_scr[q_rows[u], :] for u in todo]
                acc_old = [acc_scr[q_rows[u], :] for u in todo]
                m_new = [jnp.maximum(m_old[u], m_blk[u]) for u in todo]
                alpha = [jnp.exp(m_old[u] - m_new[u]) for u in todo]
                p = [jnp.exp(s[u] - m_new[u][:, 0:1]) for u in todo]
                l_new = [alpha[u] * l_old[u] + jnp.sum(p[u], axis=-1, keepdims=True) for u in todo]
                acc = [alpha[u] * acc_old[u] + _dot(p[u].astype(_MXU_DTYPE), vws[u]) for u in todo]
            for u in todo:
                if last:
                    o_ref[0, q_rows[u], :] = (acc[u] / l_new[u]).astype(o_ref.dtype)
                else:
                    m_scr[q_rows[u], :] = m_new[u]
                    l_scr[q_rows[u], :] = l_new[u]
                    acc_scr[q_rows[u], :] = acc[u]
            return carry

        lax.fori_loop(0, dil * nb // ATTN_BLOCKS_PER_ITER, body, 0)

    assert B_GROUPS[0][1] == 1
    for gidx, (_, dil) in enumerate(B_GROUPS):
        pl.when(gi == N_GROUPS - 1 - gidx)(functools.partial(run, dil, gidx == N_GROUPS - 1, gidx == 0))


def _attention(rel_bias, buckets, q, k, v):
    bn, _, t, _ = q.shape
    qkv_spec = pl.BlockSpec((1, 1, t, B_HEAD_DIM), lambda b, h, g: (b, (N_GROUPS - 1 - g) * B_HPG + h, 0, 0))
    return pl.pallas_call(
        functools.partial(_attn_kernel, t=t),
        grid=(bn, B_HPG, N_GROUPS),
        in_specs=[
            pl.BlockSpec(memory_space=pltpu.SMEM),
            pl.BlockSpec((1, Q_BLOCK, 2 * Q_BLOCK), lambda b, h, g: (N_GROUPS - 1 - g, 0, 0)),
            qkv_spec, qkv_spec, qkv_spec,
        ],
        out_specs=pl.BlockSpec((1, t, B_HEAD_DIM), lambda b, h, g: (b, 0, h)),
        out_shape=jax.ShapeDtypeStruct((bn, t, B_WIDTH), _ACT_DTYPE),
        scratch_shapes=[
            pltpu.VMEM((t, B_HEAD_DIM), F32),
            pltpu.VMEM((t, B_HEAD_DIM), F32),
            pltpu.VMEM((t, B_HEAD_DIM), F32),
            pltpu.VMEM((Q_BLOCK, 2 * Q_BLOCK), F32),
        ],
        compiler_params=pltpu.CompilerParams(
            dimension_semantics=("parallel", "parallel", "arbitrary"), vmem_limit_bytes=VMEM_LIMIT_BYTES),
        name="attention",
    )(rel_bias, buckets, q, k, v)


def _merge_kernel(x_ref, mod_ref, y_ref, z_ref, o_ref, pg_ref, lw_ref, lb_ref, ones_ref, wa_ref, wb_ref, wo_ref,
                  out_ref):
    y =y_ref[0, 0] + y_ref[0, 1]
    head_ones = ones_ref[...]
    mean = _head_sums(y, head_ones) * (1.0 / A_HEAD_DIM)
    yc = y - mean
    var = _head_sums(yc * yc, head_ones) * (1.0 / A_HEAD_DIM)
    yn = yc * lax.rsqrt(var + GN_EPS) * lw_ref[...] + lb_ref[...]
    ya = yn * z_ref[0, 0].astype(F32) + z_ref[0, 1].astype(F32)
    pg = pg_ref[0].astype(F32)
    yb = o_ref[0].astype(F32) * _silu(pg[:, 0:B_WIDTH])
    gate_a = _sigmoid(pg[:, B_WIDTH:B_WIDTH + D_MODEL])
    gate_b = _sigmoid(pg[:, B_WIDTH + D_MODEL:])
    merged = (gate_a * _dot(ya.astype(_MXU_DTYPE), wa_ref[...])
              + gate_b * _dot(yb.astype(_MXU_DTYPE), wb_ref[...]))
    out_ref[0] = x_ref[0] + mod_ref[0, 2:3, :] * _dot(merged.astype(_MXU_DTYPE), wo_ref[...])


def _merge(x, mod3, y, z, o, pg, lnx_w, lnx_b, wa_b, wb_b, wo_b):
    bn, t, _ = x.shape
    tm = MERGE_ROWS
    const2 = lambda b, i: (0, 0)
    return pl.pallas_call(
        _merge_kernel,
        grid=(bn, t // tm),
        in_specs=[
            pl.BlockSpec((1, tm, D_MODEL), lambda b, i: (b, i, 0)),
            pl.BlockSpec((1, 3, D_MODEL), lambda b, i: (b, 0, 0)),
            pl.BlockSpec((1, 2, tm, A_WIDTH), lambda b, i: (b, 0, i, 0)),
            pl.BlockSpec((1, 2, tm, A_WIDTH), lambda b, i: (b, 0, i, 0)),
            pl.BlockSpec((1, tm, B_WIDTH), lambda b, i: (b, i, 0)),
            pl.BlockSpec((1, tm, PG_COLS), lambda b, i: (b, i, 0)),
            pl.BlockSpec((1, A_WIDTH), const2),
            pl.BlockSpec((1, A_WIDTH), const2),
            pl.BlockSpec((A_WIDTH, A_WIDTH), const2),
            pl.BlockSpec((A_WIDTH, D_MODEL), const2),
            pl.BlockSpec((B_WIDTH, D_MODEL), const2),
            pl.BlockSpec((D_MODEL, D_MODEL), const2),
        ],
        out_specs=pl.BlockSpec((1, tm, D_MODEL), lambda b, i: (b, i, 0)),
        out_shape=jax.ShapeDtypeStruct((bn, t, D_MODEL), x.dtype),
        compiler_params=pltpu.CompilerParams(
            dimension_semantics=("parallel", "parallel"), vmem_limit_bytes=VMEM_LIMIT_BYTES),
        name="merge",
    )(x, mod3, y, z, o, pg, lnx_w.reshape(1, A_WIDTH), lnx_b.reshape(1, A_WIDTH), _head_ones(), wa_b, wb_b,
      wo_b)


def _pad_low_rank(w):
    z = jnp.zeros((LOW_RANK, A_WIDTH), w.dtype)
    return jnp.stack([jnp.concatenate([w[0], z], axis=0), jnp.concatenate([z, w[1]], axis=0)])


def kernel(x_prompt, x_sample, c_prompt, c_sample, rel_bias, norm_w, w_ada, b_ada, w_in, shift_mu, w0, w2, a0, a2,
           k_k, k_a, r_k, lnx_w, lnx_b, q_gain, k_gain, w_out_a, w_out_b, w_out):
    depth = norm_w.shape[0]
    buckets = jnp.asarray(_band_buckets())

    def trunk(x, c):
        bn, t, _ = x.shape
        assert t % (Q_BLOCK * B_GROUPS[-1][1]) == 0 and t % MERGE_ROWS == 0
        for l in range(depth):
            mod3 = _modulation(c, w_ada[l], b_ada[l]).reshape(bn, 3, D_MODEL)
            pa, q, k, v, pg = _projection(x, mod3, norm_w[l], w_in[l].astype(_MXU_DTYPE), q_gain[l], k_gain[l])
            y, z = _rwkv(pa, shift_mu[l], w0[l], _pad_low_rank(w2[l]), a0[l], _pad_low_rank(a2[l]),
                         k_k[l], k_a[l], r_k[l].reshape(A_WIDTH))
            o = _attention(rel_bias, buckets, q, k, v)
            x = _merge(x, mod3, y, z, o, pg, lnx_w[l], lnx_b[l], w_out_a[l].astype(_MXU_DTYPE),
                       w_out_b[l].astype(_MXU_DTYPE), w_out[l].astype(_MXU_DTYPE))
        return x

    return (trunk(x_prompt, c_prompt), trunk(x_sample, c_sample))
```
